```python
import jax
import jax.numpy as jnp
from jax import lax
import numpy as np

D_MODEL = 2048
BATCH = 8
SEQ = 4096
DEPTH = 4

GRID_W = 64
CTX_LEN = 256
A_WIDTH = D_MODEL // 2
A_HEAD = 64
A_HEADS = A_WIDTH // A_HEAD
DECAY_LORA = 64
ICL_LORA = 64
GATE_LORA = 128
A_PROJ = 3 * A_WIDTH + 2 * DECAY_LORA + 2 * ICL_LORA + GATE_LORA
A_SPLITS = (A_WIDTH, 2 * A_WIDTH, 3 * A_WIDTH, 3 * A_WIDTH + 2 * DECAY_LORA, 3 * A_WIDTH + 2 * DECAY_LORA + 2 * ICL_LORA)
B_WIDTH = D_MODEL - A_WIDTH
CONV_B = 31
EVEN_IN = A_PROJ + 2 * B_WIDTH
C_WIDTH = D_MODEL
CONV_C = 3
N_EXPERTS = 16
N_GROUPS = 4
EXPERTS_PER_GROUP = N_EXPERTS // N_GROUPS
TOP_K = 2
D_EXPERT = 1024
N_EVEN = (DEPTH + 1) // 2
N_ODD = DEPTH // 2
RMS_EPS = 1e-6
LN_EPS = 1e-5
GN_EPS = 64e-5

kernel_name = 'hybrid_rwkv7_conformer_shortconv_moe_dit'


def rmsnorm(x, g):
    xf = x.astype(jnp.float32)
    y = xf * lax.rsqrt(jnp.mean(xf * xf, axis=-1, keepdims=True) + RMS_EPS)
    return (y * g).astype(x.dtype)


def modulate(h, shift, scale):
    return h * (1.0 + scale) + shift


def adaln_params(cvec, w_mod, b_mod):
    return jnp.split(jax.nn.silu(cvec) @ w_mod + b_mod, 6, axis=-1)


def dwconv1d(u, w):
    width = w.shape[0]
    pad = (width - 1) // 2
    return lax.conv_general_dilated(u, w[:, None, :].astype(u.dtype), window_strides=(1,), padding=[(pad, pad)],
                                    dimension_numbers=('NWC', 'WIO', 'NWC'), feature_group_count=u.shape[-1])


def column_conv(u, w, rows):
    b, t, ch = u.shape
    cols = u.reshape(b, rows, GRID_W, ch).transpose(0, 2, 1, 3).reshape(b * GRID_W, rows, ch)
    cols = dwconv1d(cols, w)
    return cols.reshape(b, GRID_W, rows, ch).transpose(0, 2, 1, 3).reshape(b, t, ch)


def row_conv(u, w, rows):
    b, t, ch = u.shape
    return dwconv1d(u.reshape(b * rows, GRID_W, ch), w).reshape(b, t, ch)


def centred_shift(z, mu_prev, mu_next):
    z_prev = jnp.pad(z[:, :-1], ((0, 0), (1, 0), (0, 0)))
    z_next = jnp.pad(z[:, 1:], ((0, 0), (0, 1), (0, 0)))
    return z + mu_prev * (z_prev - z) + mu_next * (z_next - z)


def to_heads(t):
    return t.reshape(t.shape[:-1] + (A_HEADS, A_HEAD))


def rwkv_prepare(za, mu_prev, mu_next, w0, w2, a0, a2, g2, k_k, k_a):
    za = centred_shift(za, mu_prev, mu_next).astype(jnp.float32)
    r, k, v, dw, da, dg = jnp.split(za, A_SPLITS, axis=-1)
    lead = za.shape[:-1]
    dw = dw.reshape(lead + (2, DECAY_LORA))
    da = da.reshape(lead + (2, ICL_LORA))
    logw = -jax.nn.softplus(-(w0 + jnp.einsum('btdr,drc->btdc', jnp.tanh(dw), w2))) - 0.5
    decay = jnp.exp(-jnp.exp(logw))
    icl = jax.nn.sigmoid(a0 + jnp.einsum('btdr,drc->btdc', da, a2))
    g = jax.nn.sigmoid(dg) @ g2
    kk = to_heads(k * k_k)
    kk = kk / jnp.maximum(jnp.sqrt(jnp.sum(kk * kk, axis=-1, keepdims=True)), 1e-12)
    kdir = to_heads(k[:, :, None, :] * (1.0 + (icl - 1.0) * k_a))
    bvec = kk[:, :, None] * to_heads(icl)
    return (to_heads(r), to_heads(v), kdir, kk, to_heads(decay), bvec, g)


def wkv7_scan(s0, r, w, k, v, a, b):
    def step(s, inp):
        r_t, w_t, k_t, v_t, a_t, b_t = inp
        sa = jnp.einsum('bhvk,bhk->bhv', s, a_t)
        s = s * w_t[:, :, None, :] + sa[..., None] * b_t[:, :, None, :] + v_t[..., None] * k_t[:, :, None, :]
        return s, jnp.einsum('bhvk,bhk->bhv', s, r_t)
    xs = tuple(jnp.swapaxes(t, 0, 1) for t in (r, w, k, v, a, b))
    s, ys = lax.scan(step, s0, xs)
    return s, jnp.swapaxes(ys, 0, 1)


def directional_scan(s0, feats, d, reverse):
    r, v, kdir, kk, decay, bvec, _ = feats
    ins = (r, decay[:, :, d], kdir[:, :, d], v, -kk, bvec[:, :, d])
    if reverse:
        ins = tuple(jnp.flip(t, axis=1) for t in ins)
    s, y = wkv7_scan(s0, *ins)
    return s, (jnp.flip(y, axis=1) if reverse else y)


def rwkv_readout(y, feats, r_k, lnx_g, lnx_b):
    r, v, kdir, _, _, _, g = feats
    mean = jnp.mean(y, axis=-1, keepdims=True)
    var = jnp.var(y, axis=-1, keepdims=True)
    y = ((y - mean) * lax.rsqrt(var + GN_EPS)).reshape(y.shape[:-2] + (A_WIDTH,)) * lnx_g + lnx_b
    bonus = jnp.sum(r * (kdir[:, :, 0] + kdir[:, :, 1]) * r_k, axis=-1, keepdims=True) * v
    return (y + bonus.reshape(y.shape)) * g


def conformer_conv(zb, conv_w, conv_bias, ln_g, ln_b, rows):
    u, gate = jnp.split(zb, 2, axis=-1)
    u = u * jax.nn.sigmoid(gate)
    u = (dwconv1d(u, conv_w) if rows is None else column_conv(u, conv_w, rows)) + conv_bias
    uf = u.astype(jnp.float32)
    mean = jnp.mean(uf, axis=-1, keepdims=True)
    var = jnp.var(uf, axis=-1, keepdims=True)
    uf = (uf - mean) * lax.rsqrt(var + LN_EPS) * ln_g + ln_b
    return jax.nn.silu(uf).astype(zb.dtype)


def even_mixer(hc, hl, rows, w_in, mu_prev, mu_next, w0, w2, a0, a2, g2, k_k, k_a, r_k, lnx_g, lnx_b,
               conv_w, conv_bias, ln_g, ln_b, w_out, need_ctx_out):
    zc = hc @ w_in
    zl = hl @ w_in
    prep = (mu_prev, mu_next, w0, w2, a0, a2, g2, k_k, k_a)
    fc = rwkv_prepare(zc[..., :A_PROJ], *prep)
    fl = rwkv_prepare(zl[..., :A_PROJ], *prep)
    s0 = jnp.zeros((hc.shape[0], A_HEADS, A_HEAD, A_HEAD), jnp.float32)
    s_cf, y_cf = directional_scan(s0, fc, 0, False)
    s_cb, y_cb = directional_scan(s0, fc, 1, True)
    _, y_lf = directional_scan(s_cf, fl, 0, False)
    _, y_lb = directional_scan(s_cb, fl, 1, True)

    def merge(feats, y_f, y_b, zb, rws):
        ya = rwkv_readout(y_f + y_b, feats, r_k, lnx_g, lnx_b).astype(zb.dtype)
        yb = conformer_conv(zb, conv_w, conv_bias, ln_g, ln_b, rws)
        return jnp.concatenate([ya, yb], axis=-1) @ w_out

    out_l = merge(fl, y_lf, y_lb, zl[..., A_PROJ:], rows)
    out_c = merge(fc, y_cf, y_cb, zc[..., A_PROJ:], None) if need_ctx_out else None
    return out_c, out_l


def short_conv_mixer(h, w_in, conv_w, w_out, rows):
    bg, cg, xin = jnp.split(h @ w_in, 3, axis=-1)
    u = cg * xin
    u = dwconv1d(u, conv_w) if rows is None else row_conv(u, conv_w, rows)
    return (bg * u) @ w_out


def grouped_moe(h, w_router, router_bias, w_gate, w_up, w_down):
    n = h.shape[0]
    scores = jax.nn.sigmoid((h @ w_router).astype(jnp.float32))
    sel = scores + router_bias
    group_score = jnp.sum(lax.top_k(sel.reshape(n, N_GROUPS, EXPERTS_PER_GROUP), TOP_K)[0], axis=-1)
    best = jnp.argmax(group_score, axis=-1)
    in_group = (jnp.arange(N_EXPERTS) // EXPERTS_PER_GROUP)[None, :] == best[:, None]
    _, idx = lax.top_k(jnp.where(in_group, sel, -jnp.inf), TOP_K)
    gsel = jnp.take_along_axis(scores, idx, axis=-1)
    gsel = gsel / jnp.sum(gsel, axis=-1, keepdims=True)
    combine = jnp.sum(jax.nn.one_hot(idx, N_EXPERTS, dtype=jnp.float32) * gsel[..., None], axis=1)
    out = jnp.zeros(h.shape, jnp.float32)
    for e in range(N_EXPERTS):
        he = jax.nn.silu(h @ w_gate[e]) * (h @ w_up[e])
        out = out + combine[:, e:e + 1] * (he @ w_down[e])
    return out.astype(h.dtype)


def setup_inputs(seed: int = 0) -> dict:
    key = jax.random.key(seed)
    ks = iter(jax.random.split(key, 48))
    f32 = jnp.float32
    D = D_MODEL

    def nrm(shape, scale):
        return jax.random.normal(next(ks), shape, f32) * scale

    def gain(shape):
        return 1.0 + nrm(shape, 0.02)

    return {
        'x': nrm((BATCH, SEQ, D), 1.0),
        'c': nrm((BATCH, D), 1.0),
        'ctx': nrm((BATCH, CTX_LEN, D), 1.0),
        'c_ctx': nrm((D,), 1.0),
        'norm1_g': gain((DEPTH, D)),
        'norm2_g': gain((DEPTH, D)),
        'w_mod': nrm((DEPTH, D, 6 * D), 0.5 * D ** -0.5),
        'b_mod': nrm((DEPTH, 6 * D), 0.02),
        'w_in_e': nrm((N_EVEN, D, EVEN_IN), D ** -0.5),
        'mu_prev': jax.random.uniform(next(ks), (N_EVEN, A_PROJ), f32, 0.0, 0.5),
        'mu_next': jax.random.uniform(next(ks), (N_EVEN, A_PROJ), f32, 0.0, 0.5),
        'w0': nrm((N_EVEN, 2, A_WIDTH), 0.5),
        'w2': nrm((N_EVEN, 2, DECAY_LORA, A_WIDTH), 0.5 * DECAY_LORA ** -0.5),
        'a0': nrm((N_EVEN, 2, A_WIDTH), 0.5),
        'a2': nrm((N_EVEN, 2, ICL_LORA, A_WIDTH), 0.5 * ICL_LORA ** -0.5),
        'g2': nrm((N_EVEN, GATE_LORA, A_WIDTH), GATE_LORA ** -0.5),
        'k_k': 0.85 + nrm((N_EVEN, A_WIDTH), 0.05),
        'k_a': 1.0 + nrm((N_EVEN, A_WIDTH), 0.05),
        'r_k': nrm((N_EVEN, A_HEADS, A_HEAD), 0.1),
        'lnx_g': gain((N_EVEN, A_WIDTH)),
        'lnx_b': nrm((N_EVEN, A_WIDTH), 0.02),
        'conv_b_w': nrm((N_EVEN, CONV_B, B_WIDTH), CONV_B ** -0.5),
        'conv_b_bias': nrm((N_EVEN, B_WIDTH), 0.02),
        'lnb_g': gain((N_EVEN, B_WIDTH)),
        'lnb_b': nrm((N_EVEN, B_WIDTH), 0.02),
        'w_out_e': nrm((N_EVEN, A_WIDTH + B_WIDTH, D), (A_WIDTH + B_WIDTH) ** -0.5),
        'w_in_o': nrm((N_ODD, D, 3 * C_WIDTH), D ** -0.5),
        'conv_c_w': nrm((N_ODD, CONV_C, C_WIDTH), CONV_C ** -0.5),
        'w_out_o': nrm((N_ODD, C_WIDTH, D), C_WIDTH ** -0.5),
        'w_router': nrm((D, N_EXPERTS), D ** -0.5),
        'router_bias': nrm((N_EXPERTS,), 0.01),
        'w_gate': nrm((DEPTH, N_EXPERTS, D, D_EXPERT), D ** -0.5),
        'w_up': nrm((DEPTH, N_EXPERTS, D, D_EXPERT), D ** -0.5),
        'w_down': nrm((DEPTH, N_EXPERTS, D_EXPERT, D), D_EXPERT ** -0.5),
        'final_g': gain((D,)),
    }


def reference(x, c, ctx, c_ctx, norm1_g, norm2_g, w_mod, b_mod, w_in_e, mu_prev, mu_next, w0, w2, a0, a2, g2,
              k_k, k_a, r_k, lnx_g, lnx_b, conv_b_w, conv_b_bias, lnb_g, lnb_b, w_out_e, w_in_o, conv_c_w,
              w_out_o, w_router, router_bias, w_gate, w_up, w_down, final_g):
    b, s, d = x.shape
    n_ctx = ctx.shape[1]
    rows = s // GRID_W
    xl, xc = x, ctx
    for i in range(DEPTH):
        j = i // 2
        last = i == DEPTH - 1
        even = i % 2 == 0
        sh1_l, sc1_l, g1_l, sh2_l, sc2_l, g2_l = (t[:, None, :] for t in adaln_params(c, w_mod[i], b_mod[i]))
        hl = modulate(rmsnorm(xl, norm1_g[i]), sh1_l, sc1_l)
        if even or not last:
            sh1_c, sc1_c, g1_c, sh2_c, sc2_c, g2_c = adaln_params(c_ctx, w_mod[i], b_mod[i])
            hc = modulate(rmsnorm(xc, norm1_g[i]), sh1_c, sc1_c)
        if even:
            oc, ol = even_mixer(hc, hl, rows, w_in_e[j], mu_prev[j], mu_next[j], w0[j], w2[j], a0[j], a2[j], g2[j],
                                k_k[j], k_a[j], r_k[j], lnx_g[j], lnx_b[j], conv_b_w[j], conv_b_bias[j],
                                lnb_g[j], lnb_b[j], w_out_e[j], not last)
        else:
            ol = short_conv_mixer(hl, w_in_o[j], conv_c_w[j], w_out_o[j], rows)
            oc = None if last else short_conv_mixer(hc, w_in_o[j], conv_c_w[j], w_out_o[j], None)
        xl = xl + g1_l * ol
        h2l = modulate(rmsnorm(xl, norm2_g[i]), sh2_l, sc2_l)
        moe_w = (w_router, router_bias, w_gate[i], w_up[i], w_down[i])
        if last:
            xl = xl + g2_l * grouped_moe(h2l.reshape(b * s, d), *moe_w).reshape(b, s, d)
        else:
            xc = xc + g1_c * oc
            h2c = modulate(rmsnorm(xc, norm2_g[i]), sh2_c, sc2_c)
            o = grouped_moe(jnp.concatenate([h2c, h2l], axis=1).reshape(-1, d), *moe_w).reshape(b, n_ctx + s, d)
            xc = xc + g2_c * o[:, :n_ctx]
            xl = xl + g2_l * o[:, n_ctx:]
    return rmsnorm(xl, final_g)
```

```python
import functools

import jax
import jax.numpy as jnp
from jax import lax
from jax.experimental import pallas as pl
from jax.experimental.pallas import tpu as pltpu

F32 = jnp.float32
BF16 = jnp.bfloat16

GRID_W = 64
A_HEAD = 64
DECAY_LORA = 64
ICL_LORA = 64
GATE_LORA = 128
N_EXPERTS = 16
N_GROUPS = 4
EXPERTS_PER_GROUP = N_EXPERTS // N_GROUPS
RMS_EPS = 1e-6
LN_EPS = 1e-5
GN_EPS = 64e-5

LANE = 128
CHUNK = 64
PAIR = 2 * A_HEAD
VMEM_LIMIT = 56 * 1024 * 1024


def _cparams(sem):
    return pltpu.CompilerParams(dimension_semantics=sem, vmem_limit_bytes=VMEM_LIMIT)


def _sigmoid(x):
    return 1.0 / (1.0 + jnp.exp(-x))


def _bdot(a, b):
    return jnp.dot(a.astype(BF16), b.astype(BF16), preferred_element_type=F32)


def _rms_mod(x, g, shift, scale):
    y = x * lax.rsqrt(jnp.mean(x * x, axis=-1, keepdims=True) + RMS_EPS)
    return (y * g) * (1.0 + scale) + shift


def _segsum(x, ones_bd):
    outs = []
    for c in range(x.shape[1] // LANE):
        xs = x[:, c * LANE:(c + 1) * LANE]
        hi = xs.astype(BF16)
        lo = (xs - hi.astype(F32)).astype(BF16)
        outs.append(jnp.dot(hi, ones_bd, preferred_element_type=F32)
                    + jnp.dot(lo, ones_bd, preferred_element_type=F32))
    return jnp.concatenate(outs, axis=1)


def _adaln_kernel(cv_ref, w_ref, b_ref, o_ref):
    cv = cv_ref[...]
    s = cv * _sigmoid(cv)
    o_ref[0] = _bdot(s, w_ref[0]) + b_ref[0]


def _adaln(cv, w_mod, b_mod):
    depth, d, n = w_mod.shape
    rows = cv.shape[0]
    tn = 1024
    return pl.pallas_call(
        _adaln_kernel,
        grid=(depth, n // tn),
        in_specs=[pl.BlockSpec((rows, d), lambda i, j: (0, 0)),
                  pl.BlockSpec((1, d, tn), lambda i, j: (i, 0, j)),
                  pl.BlockSpec((1, 1, tn), lambda i, j: (i, 0, j))],
        out_specs=pl.BlockSpec((1, rows, tn), lambda i, j: (i, 0, j)),
        out_shape=jax.ShapeDtypeStruct((depth, rows, n), F32),
        compiler_params=_cparams(("parallel", "parallel")),
        name="adaln",
    )(cv, w_mod, b_mod.reshape(depth, 1, n))


def _residual_in(x_ref, d_ref, mprev_ref):
    x = x_ref[0]
    if d_ref is not None:
        x = x + mprev_ref[0, 0][5:6] * d_ref[0].astype(F32)
    return x


def _in_even_kernel(has_delta, *refs):
    if has_delta:
        x_ref, d_ref, mprev_ref, mod_ref, g_ref, w_ref, xo_ref, z_ref = refs
    else:
        x_ref, mod_ref, g_ref, w_ref, xo_ref, z_ref = refs
        d_ref = mprev_ref = None
    x = _residual_in(x_ref, d_ref, mprev_ref)
    xo_ref[0] = x
    m = mod_ref[0, 0]
    h = _rms_mod(x, g_ref[...], m[0:1], m[1:2])
    z_ref[0] = jnp.dot(h.astype(BF16), w_ref[...], preferred_element_type=F32)


def _in_odd_kernel(has_delta, n_ctx, *refs):
    if has_delta:
        x_ref, d_ref, mprev_ref, mod_ref, g_ref, w_ref, cw_ref, xo_ref, v_ref = refs
    else:
        x_ref, mod_ref, g_ref, w_ref, cw_ref, xo_ref, v_ref = refs
        d_ref = mprev_ref = None
    x = _residual_in(x_ref, d_ref, mprev_ref)
    xo_ref[0] = x
    m = mod_ref[0, 0]
    h = _rms_mod(x, g_ref[...], m[0:1], m[1:2])
    z = jnp.dot(h.astype(BF16), w_ref[...], preferred_element_type=F32)
    tm, d = x.shape
    bg = z[:, 0:d]
    u = z[:, d:2 * d] * z[:, 2 * d:3 * d]
    pos = pl.program_id(1) * tm + lax.broadcasted_iota(jnp.int32, (tm, 1), 0)
    is_ctx = pos < n_ctx
    col = lax.rem(pos - n_ctx, GRID_W)
    prev_ok = jnp.where(is_ctx, pos, col) != 0
    next_ok = jnp.where(is_ctx, pos - (n_ctx - 1), col - (GRID_W - 1)) != 0
    u_prev = jnp.where(prev_ok, pltpu.roll(u, 1, axis=0), 0.0)
    u_next = jnp.where(next_ok, pltpu.roll(u, tm - 1, axis=0), 0.0)
    cw = cw_ref[...]
    conv = cw[0:1] * u_prev + cw[1:2] * u + cw[2:3] * u_next
    v_ref[0] = (bg * conv).astype(BF16)


def _tile_rows(n_ctx, seq):
    tm = 256
    while n_ctx % tm or seq % tm:
        tm //= 2
    assert tm % GRID_W == 0 and tm % CHUNK == 0
    return tm


def _in_proj(kind, x, delta, mod_prev, mod, gain, w, conv_w, n_ctx):
    b, t, d = x.shape
    tm = _tile_rows(n_ctx, t - n_ctx)
    nct = n_ctx // tm
    n = w.shape[1]
    has_delta = delta is not None
    tok = pl.BlockSpec((1, tm, d), lambda i, j: (i, j, 0))
    modspec = pl.BlockSpec((1, 1, 6, d), lambda i, j: (i, jnp.where(j < nct, 0, 1), 0, 0))
    const2 = lambda shape: pl.BlockSpec(shape, lambda i, j: (0, 0), pipeline_mode=pl.Buffered(1))
    in_specs = [tok]
    args = [x]
    if has_delta:
        in_specs += [tok, modspec]
        args += [delta, mod_prev]
    in_specs += [modspec, const2((1, d)), const2((d, n))]
    args += [mod, gain.reshape(1, d), w]
    if kind == "even":
        body = functools.partial(_in_even_kernel, has_delta)
        out_specs = [tok, pl.BlockSpec((1, tm, n), lambda i, j: (i, j, 0))]
        out_shape = [jax.ShapeDtypeStruct((b, t, d), F32), jax.ShapeDtypeStruct((b, t, n), F32)]
    else:
        body = functools.partial(_in_odd_kernel, has_delta, n_ctx)
        in_specs.append(const2((3, d)))
        args.append(conv_w)
        out_specs = [tok, tok]
        out_shape = [jax.ShapeDtypeStruct((b, t, d), F32), jax.ShapeDtypeStruct((b, t, d), BF16)]
    return pl.pallas_call(
        body, grid=(b, t // tm), in_specs=in_specs, out_specs=out_specs, out_shape=out_shape,
        compiler_params=_cparams(("parallel", "parallel")), name="in_proj_" + kind,
    )(*args)


def _prepare_kernel(n_ctx, aw, zc_ref, zp_ref, zn_ref, mup_ref, mun_ref, w0_ref, w2_ref, a0_ref, a2_ref,
                    g2_ref, kk_ref, ka_ref, ones_ref,
                    r_ref, v_ref, kkn_ref, g_ref, ld_ref, kd_ref, icl_ref):
    z = zc_ref[0]
    tm = z.shape[0]
    j = pl.program_id(1)
    nct = n_ctx // tm
    first = jnp.logical_or(j == 0, j == nct)
    last = jnp.logical_or(j == nct - 1, j == pl.num_programs(1) - 1)
    row = lax.broadcasted_iota(jnp.int32, (tm, 1), 0)
    halo_p = jnp.where(first, 0.0, zp_ref[0][7:8])
    halo_n = jnp.where(last, 0.0, zn_ref[0][0:1])
    z_prev = jnp.where(row == 0, halo_p, pltpu.roll(z, 1, axis=0))
    z_next = jnp.where(row == tm - 1, halo_n, pltpu.roll(z, tm - 1, axis=0))
    z = z + mup_ref[...] * (z_prev - z) + mun_ref[...] * (z_next - z)
    r = z[:, 0:aw]
    k = z[:, aw:2 * aw]
    v = z[:, 2 * aw:3 * aw]
    o = 3 * aw
    dw = jnp.tanh(z[:, o:o + 2 * DECAY_LORA])
    da = z[:, o + 2 * DECAY_LORA:o + 2 * DECAY_LORA + 2 * ICL_LORA]
    dg = _sigmoid(z[:, o + 2 * DECAY_LORA + 2 * ICL_LORA:o + 2 * DECAY_LORA + 2 * ICL_LORA + GATE_LORA])
    r_ref[0] = r
    v_ref[0] = v
    g_ref[0] = _bdot(dg, g2_ref[...])
    kk = k * kk_ref[...]
    nrm = jnp.sqrt(_segsum(kk * kk, ones_ref[...]))
    kkn_ref[0] = kk / jnp.maximum(nrm, 1e-12)
    for dr in range(2):
        x = -(w0_ref[dr:dr + 1] + _bdot(dw, w2_ref[dr]))
        softplus = jnp.maximum(x, 0.0) + jnp.log(1.0 + jnp.exp(-jnp.abs(x)))
        ld_ref[dr, 0] = -jnp.exp(-softplus - 0.5)
        icl = _sigmoid(a0_ref[dr:dr + 1] + _bdot(da, a2_ref[dr]))
        icl_ref[dr, 0] = icl
        kd_ref[dr, 0] = k * (1.0 + (icl - 1.0) * ka_ref[...])


def _pad_lora(w):
    z = jnp.zeros_like(w[0])
    return jnp.stack([jnp.concatenate([w[0], z], 0), jnp.concatenate([z, w[1]], 0)]).astype(BF16)


def _ones_blockdiag():
    i = jnp.arange(LANE) // A_HEAD
    return (i[:, None] == i[None, :]).astype(BF16)


def _prepare(z, mu_prev, mu_next, w0, w2, a0, a2, g2, k_k, k_a, n_ctx):
    b, t, _ = z.shape
    aw = w0.shape[1]
    ap = mu_prev.shape[0]
    tm = _tile_rows(n_ctx, t - n_ctx)
    hb = tm // 8
    nhb = t // 8
    tok = lambda i, j: (i, j, 0)
    c2 = lambda shape: pl.BlockSpec(shape, lambda i, j: (0,) * len(shape))
    in_specs = [pl.BlockSpec((1, tm, ap), tok),
                pl.BlockSpec((1, 8, ap), lambda i, j: (i, jnp.maximum(j * hb - 1, 0), 0)),
                pl.BlockSpec((1, 8, ap), lambda i, j: (i, jnp.minimum((j + 1) * hb, nhb - 1), 0)),
                c2((1, ap)), c2((1, ap)), c2((2, aw)), c2((2, 2 * DECAY_LORA, aw)), c2((2, aw)),
                c2((2, 2 * ICL_LORA, aw)), c2((GATE_LORA, aw)), c2((1, aw)), c2((1, aw)), c2((LANE, LANE))]
    one = pl.BlockSpec((1, tm, aw), tok)
    two = pl.BlockSpec((2, 1, tm, aw), lambda i, j: (0, i, j, 0))
    s1 = jax.ShapeDtypeStruct((b, t, aw), F32)
    s2 = jax.ShapeDtypeStruct((2, b, t, aw), F32)
    return pl.pallas_call(
        functools.partial(_prepare_kernel, n_ctx, aw),
        grid=(b, t // tm), in_specs=in_specs,
        out_specs=[one, one, one, one, two, two, two],
        out_shape=[s1, s1, s1, s1, s2, s2, s2],
        compiler_params=_cparams(("parallel", "parallel")), name="rwkv_prepare",
    )(z, z, z, mu_prev.reshape(1, ap), mu_next.reshape(1, ap), w0, _pad_lora(w2), a0, _pad_lora(a2),
      g2.astype(BF16), k_k.reshape(1, aw), k_a.reshape(1, aw), _ones_blockdiag())


def _scan_kernel(n_ctx_chunks, r_ref, v_ref, kk_ref, ld_ref, kd_ref, icl_ref, y_ref, z_scr):
    dr = pl.program_id(1)
    step = pl.program_id(2)
    fwd = dr == 0
    c = CHUNK

    @pl.when(step == 0)
    def _():
        z_scr[...] = jnp.zeros_like(z_scr)

    ti = lax.broadcasted_iota(jnp.int32, (c, c), 0)
    tj = lax.broadcasted_iota(jnp.int32, (c, c), 1)
    tri = jnp.where(jnp.where(fwd, tj, ti) <= jnp.where(fwd, ti, tj), 1.0, 0.0).astype(BF16)
    ld = ld_ref[0, 0]
    ld_hi = ld.astype(BF16)
    ld_lo = (ld - ld_hi.astype(F32)).astype(BF16)
    linc = (jnp.dot(tri, ld_hi, preferred_element_type=F32)
            + jnp.dot(tri, ld_lo, preferred_element_type=F32))
    lexc = linc - ld
    ltot = jnp.where(fwd, linc[c - 1:c], linc[0:1])
    kk = kk_ref[0]
    bvec = kk * icl_ref[0, 0]
    kd = kd_ref[0, 0]
    e_neg = jnp.exp(-linc)
    e_tot = jnp.exp(ltot - linc)
    a_t = -kk * jnp.exp(lexc)
    r_t = r_ref[0] * jnp.exp(linc)
    b_t = bvec * e_neg
    k_t = kd * e_neg
    b_h = bvec * e_tot
    k_h = kd * e_tot
    p_c = jnp.exp(ltot)
    vv = v_ref[0]

    l64 = lax.broadcasted_iota(jnp.int32, (c, PAIR), 1)
    r64 = lax.broadcasted_iota(jnp.int32, (c, PAIR), 0)
    lane_a = l64 < A_HEAD
    li = jnp.where(lane_a, l64, l64 - A_HEAD)
    before = jnp.where(fwd, li, r64) < jnp.where(fwd, r64, li)
    before_eq = jnp.logical_or(before, li == r64)
    eye_pl = jnp.where(li == r64, 1.0, 0.0)
    r128 = lax.broadcasted_iota(jnp.int32, (PAIR, PAIR), 0)
    l128 = lax.broadcasted_iota(jnp.int32, (PAIR, PAIR), 1)
    diag_blk = (r128 < A_HEAD) == (l128 < A_HEAD)
    eye128 = r128 == l128
    lane_a2 = l128 < A_HEAD

    def bd(x):
        x = x.astype(F32)
        return jnp.where(diag_blk, jnp.concatenate([x, x], axis=0), 0.0).astype(BF16)

    def abd(x):
        x = x.astype(F32)
        return jnp.where(diag_blk, 0.0, jnp.concatenate([x, x], axis=0)).astype(BF16)

    nt = (((1,), (1,)), ((), ()))
    for p in range(r_ref.shape[2] // PAIR):
        sl = slice(p * PAIR, (p + 1) * PAIR)
        ar = jnp.concatenate([a_t[:, sl], r_t[:, sl]], axis=0)
        bk = jnp.concatenate([b_t[:, sl], k_t[:, sl]], axis=0).astype(BF16)
        kb = jnp.concatenate([k_t[:, sl], b_t[:, sl]], axis=0).astype(BF16)
        g1 = lax.dot_general(jnp.where(lane_a2, ar, 0.0).astype(BF16), bk, nt, preferred_element_type=F32)
        g2 = lax.dot_general(jnp.where(lane_a2, 0.0, ar).astype(BF16), kb, nt, preferred_element_type=F32)
        ab = jnp.where(jnp.logical_and(before, lane_a), g1[:c], jnp.where(before, g2[:c], 0.0))
        rb = jnp.where(jnp.logical_and(before_eq, lane_a), g1[c:], jnp.where(before_eq, g2[c:], 0.0))
        ak_s = jnp.where(jnp.logical_and(before, lane_a), g2[:c], jnp.where(before, g1[:c], 0.0))
        rk_s = jnp.where(jnp.logical_and(before_eq, lane_a), g2[c:], jnp.where(before_eq, g1[c:], 0.0))
        v_p = vv[:, sl]
        akv_rkv = jnp.dot(jnp.concatenate([ak_s, rk_s], axis=0).astype(BF16), abd(v_p),
                          preferred_element_type=F32)
        akv, rkv = akv_rkv[:c], akv_rkv[c:]
        q = ab
        s = eye_pl + ab
        q = jnp.dot(q.astype(BF16), bd(q), preferred_element_type=F32)
        n = 2
        while n < c:
            rhs = bd(s) if 2 * n >= c else jnp.concatenate([bd(q), bd(s)], axis=1)
            out = jnp.dot(q.astype(BF16), rhs, preferred_element_type=F32)
            if 2 * n >= c:
                s = s + out
            else:
                s = s + out[:, PAIR:]
                q = out[:, :PAIR]
            n *= 2
        ah_w2 = jnp.dot(s.astype(BF16), jnp.concatenate([bd(a_t[:, sl]), bd(akv)], axis=1),
                        preferred_element_type=F32)
        ah = ah_w2[:, :PAIR]
        w2 = ah_w2[:, PAIR:]
        rb_o = jnp.dot(rb.astype(BF16), jnp.concatenate([bd(ah), bd(w2)], axis=1), preferred_element_type=F32)
        rbar = r_t[:, sl] + rb_o[:, :PAIR]
        y0 = rb_o[:, PAIR:] + rkv
        bkh_t = jnp.concatenate([b_h[:, sl], k_h[:, sl]], axis=0).T.astype(BF16)
        rhs = jnp.concatenate([jnp.concatenate([ah, w2], axis=1),
                               jnp.concatenate([jnp.zeros_like(v_p), v_p], axis=1)], axis=0).astype(BF16)
        pg = jnp.dot(bkh_t, rhs, preferred_element_type=F32)
        pc_row = jnp.broadcast_to(p_c[:, sl], (PAIR, PAIR))
        phi = jnp.where(diag_blk, pg[:, :PAIR], 0.0) + jnp.where(eye128, pc_row, 0.0)
        gam = jnp.where(diag_blk, pg[:, PAIR:], 0.0)
        z0 = z_scr[p].astype(BF16)
        yz = jnp.dot(jnp.concatenate([rbar, phi], axis=0).astype(BF16), z0, preferred_element_type=F32)
        y_ref[0, 0, :, sl] = yz[:c] + y0
        z_scr[p] = yz[c:] + gam


def _scan(r, v, kkn, ld, kd, icl, n_ctx):
    b, t, aw = r.shape
    nch = t // CHUNK
    ncc = n_ctx // CHUNK

    def chunk_of(d, i):
        rev = jnp.where(i < ncc, ncc - 1 - i, nch - 1 - (i - ncc))
        return jnp.where(d == 0, i, rev)

    one = pl.BlockSpec((1, CHUNK, aw), lambda bi, d, i: (bi, chunk_of(d, i), 0))
    two = pl.BlockSpec((1, 1, CHUNK, aw), lambda bi, d, i: (d, bi, chunk_of(d, i), 0))
    return pl.pallas_call(
        functools.partial(_scan_kernel, ncc),
        grid=(b, 2, nch),
        in_specs=[one, one, one, two, two, two],
        out_specs=two,
        out_shape=jax.ShapeDtypeStruct((2, b, t, aw), F32),
        scratch_shapes=[pltpu.VMEM((aw // PAIR, PAIR, PAIR), F32)],
        compiler_params=_cparams(("parallel", "parallel", "arbitrary")), name="wkv7_scan",
    )(r, v, kkn, ld, kd, icl)


def _conv_kernel(n_ctx, width, u_ref, gate_ref, w_ref, bias_ref, o_ref, lat_scr, ctx_scr):
    half = (width - 1) // 2
    t = u_ref.shape[1]
    seq = t - n_ctx
    padl = half * GRID_W
    padc = 16
    lat_scr[0:padl] = jnp.zeros((padl, LANE), F32)
    lat_scr[padl + seq:] = jnp.zeros((padl, LANE), F32)
    lat_scr[padl:padl + seq] = u_ref[0, n_ctx:] * _sigmoid(gate_ref[0, n_ctx:])
    ctx_scr[0:padc] = jnp.zeros((padc, LANE), F32)
    ctx_scr[padc + n_ctx:] = jnp.zeros((padc, LANE), F32)
    ctx_scr[padc:padc + n_ctx] = u_ref[0, 0:n_ctx] * _sigmoid(gate_ref[0, 0:n_ctx])
    bias = bias_ref[...]
    acc = jnp.zeros((n_ctx, LANE), F32)
    for j in range(width):
        acc = acc + w_ref[j:j + 1] * ctx_scr[padc - half + j:padc - half + j + n_ctx]
    o_ref[0, 0:n_ctx] = acc + bias

    def body(rc, carry):
        acc = jnp.zeros((GRID_W, LANE), F32)
        for j in range(width):
            acc = acc + w_ref[j:j + 1] * lat_scr[pl.ds(pl.multiple_of((rc + j) * GRID_W, GRID_W), GRID_W)]
        o_ref[0, pl.ds(pl.multiple_of(n_ctx + rc * GRID_W, GRID_W), GRID_W)] = acc + bias
        return carry

    lax.fori_loop(0, seq // GRID_W, body, 0)


def _conformer_conv(z, col0, conv_w, conv_bias, n_ctx):
    b, t, _ = z.shape
    width, bw = conv_w.shape
    half = (width - 1) // 2
    nb = bw // LANE
    c0 = col0 // LANE
    return pl.pallas_call(
        functools.partial(_conv_kernel, n_ctx, width),
        grid=(b, nb),
        in_specs=[pl.BlockSpec((1, t, LANE), lambda i, j: (i, 0, c0 + j)),
                  pl.BlockSpec((1, t, LANE), lambda i, j: (i, 0, c0 + nb + j)),
                  pl.BlockSpec((width, LANE), lambda i, j: (0, j)),
                  pl.BlockSpec((1, LANE), lambda i, j: (0, j))],
        out_specs=pl.BlockSpec((1, t, LANE), lambda i, j: (i, 0, j)),
        out_shape=jax.ShapeDtypeStruct((b, t, bw), F32),
        scratch_shapes=[pltpu.VMEM((t - n_ctx + 2 * half * GRID_W, LANE), F32),
                        pltpu.VMEM((n_ctx + 32, LANE), F32)],
        compiler_params=_cparams(("parallel", "parallel")), name="conformer_conv",
    )(z, z, conv_w, conv_bias.reshape(1, bw))


def _merge_kernel(y_ref, r_ref, v_ref, kd_ref, g_ref, ub_ref, rk_ref, lg_ref, lb_ref, bg_ref, bb_ref,
                  ones_ref, o_ref):
    aw = r_ref.shape[2]
    ones = ones_ref[...]
    y = y_ref[0, 0] + y_ref[1, 0]
    inv = 1.0 / A_HEAD
    mean = _segsum(y, ones) * inv
    yc = y - mean
    var = _segsum(yc * yc, ones) * inv
    yn = yc * lax.rsqrt(var + GN_EPS) * lg_ref[...] + lb_ref[...]
    bonus = _segsum(r_ref[0] * (kd_ref[0, 0] + kd_ref[1, 0]) * rk_ref[...], ones) * v_ref[0]
    o_ref[0, :, 0:aw] = ((yn + bonus) * g_ref[0]).astype(BF16)
    u = ub_ref[0]
    mu = jnp.mean(u, axis=-1, keepdims=True)
    uc = u - mu
    uv = jnp.mean(uc * uc, axis=-1, keepdims=True)
    un = uc * lax.rsqrt(uv + LN_EPS) * bg_ref[...] + bb_ref[...]
    o_ref[0, :, aw:] = (un * _sigmoid(un)).astype(BF16)


def _merge(y, r, v, kd, g, ub, r_k, lnx_g, lnx_b, lnb_g, lnb_b, n_ctx):
    b, t, aw = r.shape
    bw = ub.shape[2]
    tm = _tile_rows(n_ctx, t - n_ctx)
    one = pl.BlockSpec((1, tm, aw), lambda i, j: (i, j, 0))
    two = pl.BlockSpec((2, 1, tm, aw), lambda i, j: (0, i, j, 0))
    c2 = lambda shape: pl.BlockSpec(shape, lambda i, j: (0, 0))
    return pl.pallas_call(
        _merge_kernel, grid=(b, t // tm),
        in_specs=[two, one, one, two, one, pl.BlockSpec((1, tm, bw), lambda i, j: (i, j, 0)),
                  c2((1, aw)), c2((1, aw)), c2((1, aw)), c2((1, bw)), c2((1, bw)), c2((LANE, LANE))],
        out_specs=pl.BlockSpec((1, tm, aw + bw), lambda i, j: (i, j, 0)),
        out_shape=jax.ShapeDtypeStruct((b, t, aw + bw), BF16),
        compiler_params=_cparams(("parallel", "parallel")), name="rwkv_readout_conformer_norm",
    )(y, r, v, kd, g, ub, r_k.reshape(1, aw), lnx_g.reshape(1, aw), lnx_b.reshape(1, aw),
      lnb_g.reshape(1, bw), lnb_b.reshape(1, bw), _ones_blockdiag())


def _out_kernel(lhs_ref, x_ref, mod_ref, g_ref, w_ref, wr_ref, rb_ref, xo_ref, h_ref, eid_ref, gate_ref):
    m = mod_ref[0, 0]
    x = x_ref[0] + m[2:3] * jnp.dot(lhs_ref[0], w_ref[...], preferred_element_type=F32)
    xo_ref[0] = x
    h = _rms_mod(x, g_ref[...], m[3:4], m[4:5]).astype(BF16)
    h_ref[0] = h
    logits = lax.dot_general(wr_ref[...], h, (((1,), (1,)), ((), ())), preferred_element_type=F32)
    scores = _sigmoid(logits)
    sel = scores + rb_ref[...]
    e = EXPERTS_PER_GROUP
    best = None
    for gi in range(N_GROUPS):
        rows = [sel[gi * e + i:gi * e + i + 1] for i in range(e)]
        top2 = None
        for i in range(e):
            for k in range(i + 1, e):
                s = rows[i] + rows[k]
                top2 = s if top2 is None else jnp.maximum(top2, s)
        if best is None:
            best, best_v = jnp.zeros_like(top2, dtype=jnp.int32), top2
        else:
            upd = top2 > best_v
            best = jnp.where(upd, gi, best)
            best_v = jnp.where(upd, top2, best_v)

    def pick(a, i):
        out = a[i:i + 1]
        for gi in range(1, N_GROUPS):
            out = jnp.where(best == gi, a[gi * e + i:gi * e + i + 1], out)
        return out

    cs = [pick(sel, i) for i in range(e)]
    sc = [pick(scores, i) for i in range(e)]

    def argmax4(vals):
        idx, val = jnp.zeros_like(best), vals[0]
        for i in range(1, e):
            upd = vals[i] > val
            idx = jnp.where(upd, i, idx)
            val = jnp.where(upd, vals[i], val)
        return idx

    i1 = argmax4(cs)
    i2 = argmax4([jnp.where(i1 == i, -jnp.inf, cs[i]) for i in range(e)])

    def take(vals, idx):
        out = vals[0]
        for i in range(1, e):
            out = jnp.where(idx == i, vals[i], out)
        return out

    s1, s2 = take(sc, i1), take(sc, i2)
    tot = s1 + s2
    eid_ref[0] = jnp.concatenate([best * e + i1, best * e + i2], axis=0)
    gate_ref[0] = jnp.concatenate([s1 / tot, s2 / tot], axis=0)


def _out_proj(lhs, x, mod, gain, w, w_router, router_bias, n_ctx):
    b, t, d = x.shape
    tm = _tile_rows(n_ctx, t - n_ctx)
    nct = n_ctx // tm
    k = lhs.shape[2]
    ne = w_router.shape[1]
    tok = lambda i, j: (i, j, 0)
    c2 = lambda shape: pl.BlockSpec(shape, lambda i, j: (0, 0), pipeline_mode=pl.Buffered(1))
    return pl.pallas_call(
        _out_kernel, grid=(b, t // tm),
        in_specs=[pl.BlockSpec((1, tm, k), tok), pl.BlockSpec((1, tm, d), tok),
                  pl.BlockSpec((1, 1, 6, d), lambda i, j: (i, jnp.where(j < nct, 0, 1), 0, 0)),
                  c2((1, d)), c2((k, d)), c2((ne, d)), c2((ne, 1))],
        out_specs=[pl.BlockSpec((1, tm, d), tok), pl.BlockSpec((1, tm, d), tok),
                   pl.BlockSpec((1, 2, tm), lambda i, j: (i, 0, j)),
                   pl.BlockSpec((1, 2, tm), lambda i, j: (i, 0, j))],
        out_shape=[jax.ShapeDtypeStruct((b, t, d), F32), jax.ShapeDtypeStruct((b, t, d), BF16),
                   jax.ShapeDtypeStruct((b, 2, t), jnp.int32), jax.ShapeDtypeStruct((b, 2, t), F32)],
        compiler_params=_cparams(("parallel", "parallel")), name="out_proj_router",
    )(lhs, x, mod, gain.reshape(1, d), w, w_router.T.astype(BF16), router_bias.reshape(ne, 1))


def _expert_kernel(te_ref, nu_ref, x_ref, gt_ref, wg_ref, wu_ref, wd_ref, o_ref):
    i = pl.program_id(0)

    @pl.when(i < nu_ref[0])
    def _():
        x = x_ref[...]
        a = jnp.dot(x, wg_ref[0], preferred_element_type=F32)
        u = jnp.dot(x, wu_ref[0], preferred_element_type=F32)
        he = (a * _sigmoid(a) * u).astype(BF16)
        y = jnp.dot(he, wd_ref[0], preferred_element_type=F32)
        o_ref[...] = (y * gt_ref[...]).astype(BF16)

    @pl.when(i >= nu_ref[0])
    def _():
        o_ref[...] = jnp.zeros_like(o_ref)


def _experts(x_sorted, gate_sorted, tile_expert, n_used, w_gate, w_up, w_down, tm):
    p, d = x_sorted.shape
    f = w_gate.shape[2]
    grid_spec = pltpu.PrefetchScalarGridSpec(
        num_scalar_prefetch=2, grid=(p // tm,),
        in_specs=[pl.BlockSpec((tm, d), lambda i, te, nu: (i, 0)),
                  pl.BlockSpec((tm, 1), lambda i, te, nu: (i, 0)),
                  pl.BlockSpec((1, d, f), lambda i, te, nu: (te[i], 0, 0)),
                  pl.BlockSpec((1, d, f), lambda i, te, nu: (te[i], 0, 0)),
                  pl.BlockSpec((1, f, d), lambda i, te, nu: (te[i], 0, 0))],
        out_specs=pl.BlockSpec((tm, d), lambda i, te, nu: (i, 0)))
    return pl.pallas_call(
        _expert_kernel, grid_spec=grid_spec,
        out_shape=jax.ShapeDtypeStruct((p, d), BF16),
        compiler_params=_cparams(("arbitrary",)), name="moe_experts",
    )(tile_expert, n_used, x_sorted, gate_sorted, w_gate, w_up, w_down)


def _moe(h2, eid, gates, w_gate, w_up, w_down, tm):
    b, t, d = h2.shape
    n = b * t
    ne = w_gate.shape[0]
    e_flat = eid.transpose(0, 2, 1).reshape(n * 2)
    onehot = (e_flat[:, None] == jnp.arange(ne, dtype=jnp.int32)[None, :]).astype(jnp.int32)
    csum = jnp.cumsum(onehot, axis=0)
    counts = csum[-1]
    rank = jnp.sum((csum - 1) * onehot, axis=1)
    padded = ((counts + tm - 1) // tm) * tm
    ends = jnp.cumsum(padded)
    starts = ends - padded
    pos = jnp.sum(onehot * starts[None, :], axis=1) + rank
    n_tiles = (2 * n) // tm + ne
    p = n_tiles * tm
    tok_of_slot = jnp.arange(2 * n, dtype=jnp.int32) // 2
    src = jnp.zeros((p,), jnp.int32).at[pos].set(tok_of_slot)
    gate_sorted = jnp.zeros((p,), F32).at[pos].set(gates.transpose(0, 2, 1).reshape(n * 2))
    tile_start = jnp.arange(n_tiles, dtype=jnp.int32) * tm
    n_used = (ends[-1] // tm).astype(jnp.int32)
    tile_expert = jnp.minimum(jnp.searchsorted(ends, tile_start, side="right"), ne - 1).astype(jnp.int32)
    last_used = tile_expert[jnp.maximum(n_used - 1, 0)]
    tile_expert = jnp.where(jnp.arange(n_tiles) < n_used, tile_expert, last_used)
    x_sorted = jnp.take(h2.reshape(n, d), src, axis=0)
    y_sorted = _experts(x_sorted, gate_sorted.reshape(p, 1), tile_expert, n_used.reshape(1),
                        w_gate, w_up, w_down, tm)
    pos2 = pos.reshape(n, 2)
    out = (jnp.take(y_sorted, pos2[:, 0], axis=0).astype(F32)
           + jnp.take(y_sorted, pos2[:, 1], axis=0).astype(F32))
    return out.astype(BF16).reshape(b, t, d)


def _final_kernel(x_ref, d_ref, mprev_ref, g_ref, o_ref):
    x = x_ref[0] + mprev_ref[0, 0][5:6] * d_ref[0].astype(F32)
    o_ref[0] = x * lax.rsqrt(jnp.mean(x * x, axis=-1, keepdims=True) + RMS_EPS) * g_ref[...]


def _final(x, delta, mod_prev, gain, n_ctx):
    b, t, d = x.shape
    tm = _tile_rows(n_ctx, t - n_ctx)
    nct = n_ctx // tm
    tok = pl.BlockSpec((1, tm, d), lambda i, j: (i, j + nct, 0))
    return pl.pallas_call(
        _final_kernel, grid=(b, (t - n_ctx) // tm),
        in_specs=[tok, tok, pl.BlockSpec((1, 1, 6, d), lambda i, j: (i, 1, 0, 0)),
                  pl.BlockSpec((1, d), lambda i, j: (0, 0))],
        out_specs=pl.BlockSpec((1, tm, d), lambda i, j: (i, j, 0)),
        out_shape=jax.ShapeDtypeStruct((b, t - n_ctx, d), F32),
        compiler_params=_cparams(("parallel", "parallel")), name="final_norm",
    )(x, delta, mod_prev, gain.reshape(1, d))


def kernel(x, c, ctx, c_ctx, norm1_g, norm2_g, w_mod, b_mod, w_in_e, mu_prev, mu_next, w0, w2, a0, a2, g2,
           k_k, k_a, r_k, lnx_g, lnx_b, conv_b_w, conv_b_bias, lnb_g, lnb_b, w_out_e, w_in_o, conv_c_w,
           w_out_o, w_router, router_bias, w_gate, w_up, w_down, final_g):
    b, s, d = x.shape
    n_ctx = ctx.shape[1]
    depth = w_mod.shape[0]
    tm = _tile_rows(n_ctx, s)
    aw = w0.shape[2]
    a_proj = mu_prev.shape[1]

    xa = jnp.concatenate([ctx, x], axis=1)
    rows = ((b + 1 + 7) // 8) * 8
    cv = jnp.zeros((rows, d), F32).at[:b].set(c).at[b].set(c_ctx)
    mods_all = _adaln(cv, w_mod, b_mod).reshape(depth, rows, 6, d)
    mods = [jnp.stack([jnp.broadcast_to(mods_all[i, b], (b, 6, d)), mods_all[i, :b]], axis=1)
            for i in range(depth)]

    delta = None
    for i in range(depth):
        j = i // 2
        mod_prev = mods[i - 1] if i else None
        if i % 2 == 0:
            xa, z = _in_proj("even", xa, delta, mod_prev, mods[i], norm1_g[i], w_in_e[j].astype(BF16), None,
                             n_ctx)
            r, v, kkn, g, ld, kd, icl = _prepare(z, mu_prev[j], mu_next[j], w0[j], w2[j], a0[j], a2[j], g2[j],
                                                 k_k[j], k_a[j], n_ctx)
            y = _scan(r, v, kkn, ld, kd, icl, n_ctx)
            ub = _conformer_conv(z, a_proj, conv_b_w[j], conv_b_bias[j], n_ctx)
            lhs = _merge(y, r, v, kd, g, ub, r_k[j].reshape(aw), lnx_g[j], lnx_b[j], lnb_g[j], lnb_b[j], n_ctx)
            w_out = w_out_e[j]
        else:
            xa, lhs = _in_proj("odd", xa, delta, mod_prev, mods[i], norm1_g[i], w_in_o[j].astype(BF16),
                               conv_c_w[j], n_ctx)
            w_out = w_out_o[j]
        xa, h2, eid, gates = _out_proj(lhs, xa, mods[i], norm2_g[i], w_out.astype(BF16), w_router, router_bias,
                                       n_ctx)
        delta = _moe(h2, eid, gates, w_gate[i].astype(BF16), w_up[i].astype(BF16), w_down[i].astype(BF16), tm)
    return _final(xa, delta, mods[depth - 1], final_g, n_ctx)
```

```python
import functools

import jax
import jax.numpy as jnp
from jax import lax
from jax.experimental import pallas as pl
from jax.experimental.pallas import tpu as pltpu

F32 = jnp.float32
BF16 = jnp.bfloat16

GRID_W = 64
A_HEAD = 64
DECAY_LORA = 64
ICL_LORA = 64
GATE_LORA = 128
N_EXPERTS = 16
N_GROUPS = 4
EXPERTS_PER_GROUP = N_EXPERTS // N_GROUPS
RMS_EPS = 1e-6
LN_EPS = 1e-5
GN_EPS = 64e-5

LANE = 128
CHUNK = 64
PAIR = 2 * A_HEAD
VMEM_LIMIT = 56 * 1024 * 1024


def _cparams(sem):
    return pltpu.CompilerParams(dimension_semantics=sem, vmem_limit_bytes=VMEM_LIMIT)


def _sigmoid(x):
    return 1.0 / (1.0 + jnp.exp(-x))


def _bdot(a, b):
    return jnp.dot(a.astype(BF16), b.astype(BF16), preferred_element_type=F32)


def _rms_mod(x, g, shift, scale):
    y = x * lax.rsqrt(jnp.mean(x * x, axis=-1, keepdims=True) + RMS_EPS)
    return (y * g) * (1.0 + scale) + shift


def _segsum(x, ones_bd):
    outs = []
    for c in range(x.shape[1] // LANE):
        xs = x[:, c * LANE:(c + 1) * LANE]
        hi = xs.astype(BF16)
        lo = (xs - hi.astype(F32)).astype(BF16)
        outs.append(jnp.dot(hi, ones_bd, preferred_element_type=F32)
                    + jnp.dot(lo, ones_bd, preferred_element_type=F32))
    return jnp.concatenate(outs, axis=1)


def _adaln_kernel(cv_ref, w_ref, b_ref, o_ref):
    cv = cv_ref[...]
    s = cv * _sigmoid(cv)
    o_ref[0] = _bdot(s, w_ref[0]) + b_ref[0]


def _adaln(cv, w_mod, b_mod):
    depth, d, n = w_mod.shape
    rows = cv.shape[0]
    tn = 1024
    return pl.pallas_call(
        _adaln_kernel,
        grid=(depth, n // tn),
        in_specs=[pl.BlockSpec((rows, d), lambda i, j: (0, 0)),
                  pl.BlockSpec((1, d, tn), lambda i, j: (i, 0, j)),
                  pl.BlockSpec((1, 1, tn), lambda i, j: (i, 0, j))],
        out_specs=pl.BlockSpec((1, rows, tn), lambda i, j: (i, 0, j)),
        out_shape=jax.ShapeDtypeStruct((depth, rows, n), F32),
        compiler_params=_cparams(("parallel", "parallel")),
        name="adaln",
    )(cv, w_mod, b_mod.reshape(depth, 1, n))


def _residual_in(x_ref, moe_refs):
    x = x_ref[0]
    if moe_refs is not None:
        y1_ref, y2_ref, gt_ref, mprev_ref = moe_refs
        gt = gt_ref[0]
        moe = gt[:, 0:1] * y1_ref[0].astype(F32) + gt[:, 1:2] * y2_ref[0].astype(F32)
        x = x + mprev_ref[0, 0][5:6] * moe
    return x


def _in_even_kernel(has_delta, *refs):
    x_ref, moe_refs = refs[0], None
    if has_delta:
        moe_refs, refs = refs[1:5], refs[:1] + refs[5:]
    _, mod_ref, g_ref, w_ref, xo_ref, z_ref = refs
    x = _residual_in(x_ref, moe_refs)
    xo_ref[0] = x
    m = mod_ref[0, 0]
    h = _rms_mod(x, g_ref[...], m[0:1], m[1:2])
    z_ref[0] = jnp.dot(h.astype(BF16), w_ref[...], preferred_element_type=F32)


def _in_odd_kernel(has_delta, n_ctx, *refs):
    x_ref, moe_refs = refs[0], None
    if has_delta:
        moe_refs, refs = refs[1:5], refs[:1] + refs[5:]
    _, mod_ref, g_ref, w_ref, cw_ref, xo_ref, v_ref = refs
    x = _residual_in(x_ref, moe_refs)
    xo_ref[0] = x
    m = mod_ref[0, 0]
    h = _rms_mod(x, g_ref[...], m[0:1], m[1:2])
    z = jnp.dot(h.astype(BF16), w_ref[...], preferred_element_type=F32)
    tm, d = x.shape
    bg = z[:, 0:d]
    u = z[:, d:2 * d] * z[:, 2 * d:3 * d]
    pos = pl.program_id(1) * tm + lax.broadcasted_iota(jnp.int32, (tm, 1), 0)
    is_ctx = pos < n_ctx
    col = lax.rem(pos - n_ctx, GRID_W)
    prev_ok = jnp.where(is_ctx, pos, col) != 0
    next_ok = jnp.where(is_ctx, pos - (n_ctx - 1), col - (GRID_W - 1)) != 0
    u_prev = jnp.where(prev_ok, pltpu.roll(u, 1, axis=0), 0.0)
    u_next = jnp.where(next_ok, pltpu.roll(u, tm - 1, axis=0), 0.0)
    cw = cw_ref[...]
    conv = cw[0:1] * u_prev + cw[1:2] * u + cw[2:3] * u_next
    v_ref[0] = (bg * conv).astype(BF16)


def _tile_rows(n_ctx, seq):
    tm = 256
    while n_ctx % tm or seq % tm:
        tm //= 2
    assert tm % GRID_W == 0 and tm % CHUNK == 0
    return tm


def _in_proj(kind, x, delta, mod_prev, mod, gain, w, conv_w, n_ctx):
    b, t, d = x.shape
    tm = _tile_rows(n_ctx, t - n_ctx)
    nct = n_ctx // tm
    n = w.shape[1]
    has_delta = delta is not None
    tok = pl.BlockSpec((1, tm, d), lambda i, j: (i, j, 0))
    modspec = pl.BlockSpec((1, 1, 6, d), lambda i, j: (i, jnp.where(j < nct, 0, 1), 0, 0))
    const2 = lambda shape: pl.BlockSpec(shape, lambda i, j: (0, 0), pipeline_mode=pl.Buffered(1))
    in_specs = [tok]
    args = [x]
    if has_delta:
        in_specs += [tok, tok, pl.BlockSpec((1, tm, 2), lambda i, j: (i, j, 0)), modspec]
        args += [*delta, mod_prev]
    in_specs += [modspec, const2((1, d)), const2((d, n))]
    args += [mod, gain.reshape(1, d), w]
    if kind == "even":
        body = functools.partial(_in_even_kernel, has_delta)
        out_specs = [tok, pl.BlockSpec((1, tm, n), lambda i, j: (i, j, 0))]
        out_shape = [jax.ShapeDtypeStruct((b, t, d), F32), jax.ShapeDtypeStruct((b, t, n), F32)]
    else:
        body = functools.partial(_in_odd_kernel, has_delta, n_ctx)
        in_specs.append(const2((3, d)))
        args.append(conv_w)
        out_specs = [tok, tok]
        out_shape = [jax.ShapeDtypeStruct((b, t, d), F32), jax.ShapeDtypeStruct((b, t, d), BF16)]
    return pl.pallas_call(
        body, grid=(b, t // tm), in_specs=in_specs, out_specs=out_specs, out_shape=out_shape,
        compiler_params=_cparams(("parallel", "parallel")), name="in_proj_" + kind,
    )(*args)


def _prepare_kernel(n_ctx, aw, zc_ref, zp_ref, zn_ref, mup_ref, mun_ref, w0_ref, w2_ref, a0_ref, a2_ref,
                    g2_ref, kk_ref, ka_ref, ones_ref,
                    r_ref, v_ref, kkn_ref, g_ref, ld_ref, kd_ref, icl_ref):
    z = zc_ref[0]
    tm = z.shape[0]
    j = pl.program_id(1)
    nct = n_ctx // tm
    first = jnp.logical_or(j == 0, j == nct)
    last = jnp.logical_or(j == nct - 1, j == pl.num_programs(1) - 1)
    row = lax.broadcasted_iota(jnp.int32, (tm, 1), 0)
    halo_p = jnp.where(first, 0.0, zp_ref[0][7:8])
    halo_n = jnp.where(last, 0.0, zn_ref[0][0:1])
    z_prev = jnp.where(row == 0, halo_p, pltpu.roll(z, 1, axis=0))
    z_next = jnp.where(row == tm - 1, halo_n, pltpu.roll(z, tm - 1, axis=0))
    z = z + mup_ref[...] * (z_prev - z) + mun_ref[...] * (z_next - z)
    r = z[:, 0:aw]
    k = z[:, aw:2 * aw]
    v = z[:, 2 * aw:3 * aw]
    o = 3 * aw
    dw = jnp.tanh(z[:, o:o + 2 * DECAY_LORA])
    da = z[:, o + 2 * DECAY_LORA:o + 2 * DECAY_LORA + 2 * ICL_LORA]
    dg = _sigmoid(z[:, o + 2 * DECAY_LORA + 2 * ICL_LORA:o + 2 * DECAY_LORA + 2 * ICL_LORA + GATE_LORA])
    r_ref[0] = r
    v_ref[0] = v
    g_ref[0] = _bdot(dg, g2_ref[...])
    kk = k * kk_ref[...]
    nrm = jnp.sqrt(_segsum(kk * kk, ones_ref[...]))
    kkn_ref[0] = kk / jnp.maximum(nrm, 1e-12)
    for dr in range(2):
        x = -(w0_ref[dr:dr + 1] + _bdot(dw, w2_ref[dr]))
        softplus = jnp.maximum(x, 0.0) + jnp.log(1.0 + jnp.exp(-jnp.abs(x)))
        ld_ref[dr, 0] = -jnp.exp(-softplus - 0.5)
        icl = _sigmoid(a0_ref[dr:dr + 1] + _bdot(da, a2_ref[dr]))
        icl_ref[dr, 0] = icl
        kd_ref[dr, 0] = k * (1.0 + (icl - 1.0) * ka_ref[...])


def _pad_lora(w):
    z = jnp.zeros_like(w[0])
    return jnp.stack([jnp.concatenate([w[0], z], 0), jnp.concatenate([z, w[1]], 0)]).astype(BF16)


def _ones_blockdiag():
    i = jnp.arange(LANE) // A_HEAD
    return (i[:, None] == i[None, :]).astype(BF16)


def _prepare(z, mu_prev, mu_next, w0, w2, a0, a2, g2, k_k, k_a, n_ctx):
    b, t, _ = z.shape
    aw = w0.shape[1]
    ap = mu_prev.shape[0]
    tm = _tile_rows(n_ctx, t - n_ctx)
    hb = tm // 8
    nhb = t // 8
    tok = lambda i, j: (i, j, 0)
    c2 = lambda shape: pl.BlockSpec(shape, lambda i, j: (0,) * len(shape))
    in_specs = [pl.BlockSpec((1, tm, ap), tok),
                pl.BlockSpec((1, 8, ap), lambda i, j: (i, jnp.maximum(j * hb - 1, 0), 0)),
                pl.BlockSpec((1, 8, ap), lambda i, j: (i, jnp.minimum((j + 1) * hb, nhb - 1), 0)),
                c2((1, ap)), c2((1, ap)), c2((2, aw)), c2((2, 2 * DECAY_LORA, aw)), c2((2, aw)),
                c2((2, 2 * ICL_LORA, aw)), c2((GATE_LORA, aw)), c2((1, aw)), c2((1, aw)), c2((LANE, LANE))]
    one = pl.BlockSpec((1, tm, aw), tok)
    two = pl.BlockSpec((2, 1, tm, aw), lambda i, j: (0, i, j, 0))
    s1 = jax.ShapeDtypeStruct((b, t, aw), F32)
    s2 = jax.ShapeDtypeStruct((2, b, t, aw), F32)
    return pl.pallas_call(
        functools.partial(_prepare_kernel, n_ctx, aw),
        grid=(b, t // tm), in_specs=in_specs,
        out_specs=[one, one, one, one, two, two, two],
        out_shape=[s1, s1, s1, s1, s2, s2, s2],
        compiler_params=_cparams(("parallel", "parallel")), name="rwkv_prepare",
    )(z, z, z, mu_prev.reshape(1, ap), mu_next.reshape(1, ap), w0, _pad_lora(w2), a0, _pad_lora(a2),
      g2.astype(BF16), k_k.reshape(1, aw), k_a.reshape(1, aw), _ones_blockdiag())


def _scan_kernel(n_ctx_chunks, r_ref, v_ref, kk_ref, ld_ref, kd_ref, icl_ref, y_ref, z_scr):
    dr = pl.program_id(1)
    step = pl.program_id(2)
    fwd = dr == 0
    c = CHUNK

    @pl.when(step == 0)
    def _():
        z_scr[...] = jnp.zeros_like(z_scr)

    ti = lax.broadcasted_iota(jnp.int32, (c, c), 0)
    tj = lax.broadcasted_iota(jnp.int32, (c, c), 1)
    tri = jnp.where(jnp.where(fwd, tj, ti) <= jnp.where(fwd, ti, tj), 1.0, 0.0).astype(BF16)
    ld = ld_ref[0, 0]
    ld_hi = ld.astype(BF16)
    ld_lo = (ld - ld_hi.astype(F32)).astype(BF16)
    linc = (jnp.dot(tri, ld_hi, preferred_element_type=F32)
            + jnp.dot(tri, ld_lo, preferred_element_type=F32))
    lexc = linc - ld
    ltot = jnp.where(fwd, linc[c - 1:c], linc[0:1])
    kk = kk_ref[0]
    bvec = kk * icl_ref[0, 0]
    kd = kd_ref[0, 0]
    e_neg = jnp.exp(-linc)
    e_tot = jnp.exp(ltot - linc)
    a_t = -kk * jnp.exp(lexc)
    r_t = r_ref[0] * jnp.exp(linc)
    b_t = bvec * e_neg
    k_t = kd * e_neg
    b_h = bvec * e_tot
    k_h = kd * e_tot
    p_c = jnp.exp(ltot)
    vv = v_ref[0]

    l64 = lax.broadcasted_iota(jnp.int32, (c, PAIR), 1)
    r64 = lax.broadcasted_iota(jnp.int32, (c, PAIR), 0)
    lane_a = l64 < A_HEAD
    li = jnp.where(lane_a, l64, l64 - A_HEAD)
    before = jnp.where(fwd, li, r64) < jnp.where(fwd, r64, li)
    before_eq = jnp.logical_or(before, li == r64)
    eye_pl = jnp.where(li == r64, 1.0, 0.0)
    r128 = lax.broadcasted_iota(jnp.int32, (PAIR, PAIR), 0)
    l128 = lax.broadcasted_iota(jnp.int32, (PAIR, PAIR), 1)
    diag_blk = (r128 < A_HEAD) == (l128 < A_HEAD)
    eye128 = r128 == l128
    lane_a2 = l128 < A_HEAD

    def bd(x):
        x = x.astype(F32)
        return jnp.where(diag_blk, jnp.concatenate([x, x], axis=0), 0.0).astype(BF16)

    def abd(x):
        x = x.astype(F32)
        return jnp.where(diag_blk, 0.0, jnp.concatenate([x, x], axis=0)).astype(BF16)

    nt = (((1,), (1,)), ((), ()))
    pairs = range(r_ref.shape[2] // PAIR)
    sls = [slice(p * PAIR, (p + 1) * PAIR) for p in pairs]
    mm = lambda a, b: jnp.dot(a.astype(BF16), b, preferred_element_type=F32)
    g1, g2 = [], []
    for sl in sls:
        ar = jnp.concatenate([a_t[:, sl], r_t[:, sl]], axis=0)
        bk = jnp.concatenate([b_t[:, sl], k_t[:, sl]], axis=0).astype(BF16)
        kb = jnp.concatenate([k_t[:, sl], b_t[:, sl]], axis=0).astype(BF16)
        g1.append(lax.dot_general(jnp.where(lane_a2, ar, 0.0).astype(BF16), bk, nt, preferred_element_type=F32))
        g2.append(lax.dot_general(jnp.where(lane_a2, 0.0, ar).astype(BF16), kb, nt, preferred_element_type=F32))
    sel_a = jnp.logical_and(before, lane_a)
    sel_a_eq = jnp.logical_and(before_eq, lane_a)
    ab = [jnp.where(sel_a, x[:c], jnp.where(before, y[:c], 0.0)) for x, y in zip(g1, g2)]
    rb = [jnp.where(sel_a_eq, x[c:], jnp.where(before_eq, y[c:], 0.0)) for x, y in zip(g1, g2)]
    ak_s = [jnp.where(sel_a, y[:c], jnp.where(before, x[:c], 0.0)) for x, y in zip(g1, g2)]
    rk_s = [jnp.where(sel_a_eq, y[c:], jnp.where(before_eq, x[c:], 0.0)) for x, y in zip(g1, g2)]
    akv_rkv = [mm(jnp.concatenate([a, b], axis=0), abd(vv[:, sl])) for a, b, sl in zip(ak_s, rk_s, sls)]
    s = [eye_pl + x for x in ab]
    q = [mm(x, bd(x)) for x in ab]
    n = 2
    while n < c:
        if 2 * n >= c:
            s = [si + mm(qi, bd(si)) for qi, si in zip(q, s)]
        else:
            out = [mm(qi, jnp.concatenate([bd(qi), bd(si)], axis=1)) for qi, si in zip(q, s)]
            s = [si + o[:, PAIR:] for si, o in zip(s, out)]
            q = [o[:, :PAIR] for o in out]
        n *= 2
    ah_w2 = [mm(si, jnp.concatenate([bd(a_t[:, sl]), bd(x[:c])], axis=1)) for si, sl, x in zip(s, sls, akv_rkv)]
    rb_o = [mm(x, jnp.concatenate([bd(o[:, :PAIR]), bd(o[:, PAIR:])], axis=1)) for x, o in zip(rb, ah_w2)]
    pg = []
    for sl, o in zip(sls, ah_w2):
        v_p = vv[:, sl]
        bkh_t = jnp.concatenate([b_h[:, sl], k_h[:, sl]], axis=0).T
        rhs = jnp.concatenate([o, jnp.concatenate([jnp.zeros_like(v_p), v_p], axis=1)], axis=0)
        pg.append(mm(bkh_t, rhs.astype(BF16)))
    ys, zs = [], []
    for p, sl in zip(pairs, sls):
        rbar = r_t[:, sl] + rb_o[p][:, :PAIR]
        y0 = rb_o[p][:, PAIR:] + akv_rkv[p][c:]
        pc_row = jnp.broadcast_to(p_c[:, sl], (PAIR, PAIR))
        phi = jnp.where(diag_blk, pg[p][:, :PAIR], 0.0) + jnp.where(eye128, pc_row, 0.0)
        gam = jnp.where(diag_blk, pg[p][:, PAIR:], 0.0)
        yz = mm(jnp.concatenate([rbar, phi], axis=0), z_scr[p].astype(BF16))
        ys.append(yz[:c] + y0)
        zs.append(yz[c:] + gam)
    y_ref[0, 0] = jnp.concatenate(ys, axis=1)
    for p in pairs:
        z_scr[p] = zs[p]


def _scan(r, v, kkn, ld, kd, icl, n_ctx):
    b, t, aw = r.shape
    nch = t // CHUNK
    ncc = n_ctx // CHUNK

    def chunk_of(d, i):
        rev = jnp.where(i < ncc, ncc - 1 - i, nch - 1 - (i - ncc))
        return jnp.where(d == 0, i, rev)

    one = pl.BlockSpec((1, CHUNK, aw), lambda bi, d, i: (bi, chunk_of(d, i), 0))
    two = pl.BlockSpec((1, 1, CHUNK, aw), lambda bi, d, i: (d, bi, chunk_of(d, i), 0))
    return pl.pallas_call(
        functools.partial(_scan_kernel, ncc),
        grid=(b, 2, nch),
        in_specs=[one, one, one, two, two, two],
        out_specs=two,
        out_shape=jax.ShapeDtypeStruct((2, b, t, aw), F32),
        scratch_shapes=[pltpu.VMEM((aw // PAIR, PAIR, PAIR), F32)],
        compiler_params=_cparams(("parallel", "parallel", "arbitrary")), name="wkv7_scan",
    )(r, v, kkn, ld, kd, icl)


def _conv_kernel(n_ctx, width, u_ref, gate_ref, w_ref, bias_ref, o_ref, lat_scr, ctx_scr):
    half = (width - 1) // 2
    t = u_ref.shape[1]
    seq = t - n_ctx
    padl = half * GRID_W
    padc = 16
    lat_scr[0:padl] = jnp.zeros((padl, LANE), F32)
    lat_scr[padl + seq:] = jnp.zeros((padl, LANE), F32)
    lat_scr[padl:padl + seq] = u_ref[0, n_ctx:] * _sigmoid(gate_ref[0, n_ctx:])
    ctx_scr[0:padc] = jnp.zeros((padc, LANE), F32)
    ctx_scr[padc + n_ctx:] = jnp.zeros((padc, LANE), F32)
    ctx_scr[padc:padc + n_ctx] = u_ref[0, 0:n_ctx] * _sigmoid(gate_ref[0, 0:n_ctx])
    bias = bias_ref[...]
    acc = jnp.zeros((n_ctx, LANE), F32)
    for j in range(width):
        acc = acc + w_ref[j:j + 1] * ctx_scr[padc - half + j:padc - half + j + n_ctx]
    o_ref[0, 0:n_ctx] = acc + bias

    def body(rc, carry):
        acc = jnp.zeros((GRID_W, LANE), F32)
        for j in range(width):
            acc = acc + w_ref[j:j + 1] * lat_scr[pl.ds(pl.multiple_of((rc + j) * GRID_W, GRID_W), GRID_W)]
        o_ref[0, pl.ds(pl.multiple_of(n_ctx + rc * GRID_W, GRID_W), GRID_W)] = acc + bias
        return carry

    lax.fori_loop(0, seq // GRID_W, body, 0)


def _conformer_conv(z, col0, conv_w, conv_bias, n_ctx):
    b, t, _ = z.shape
    width, bw = conv_w.shape
    half = (width - 1) // 2
    nb = bw // LANE
    c0 = col0 // LANE
    return pl.pallas_call(
        functools.partial(_conv_kernel, n_ctx, width),
        grid=(b, nb),
        in_specs=[pl.BlockSpec((1, t, LANE), lambda i, j: (i, 0, c0 + j)),
                  pl.BlockSpec((1, t, LANE), lambda i, j: (i, 0, c0 + nb + j)),
                  pl.BlockSpec((width, LANE), lambda i, j: (0, j)),
                  pl.BlockSpec((1, LANE), lambda i, j: (0, j))],
        out_specs=pl.BlockSpec((1, t, LANE), lambda i, j: (i, 0, j)),
        out_shape=jax.ShapeDtypeStruct((b, t, bw), F32),
        scratch_shapes=[pltpu.VMEM((t - n_ctx + 2 * half * GRID_W, LANE), F32),
                        pltpu.VMEM((n_ctx + 32, LANE), F32)],
        compiler_params=_cparams(("parallel", "parallel")), name="conformer_conv",
    )(z, z, conv_w, conv_bias.reshape(1, bw))


def _merge_kernel(y_ref, r_ref, v_ref, kd_ref, g_ref, ub_ref, rk_ref, lg_ref, lb_ref, bg_ref, bb_ref,
                  ones_ref, o_ref):
    aw = r_ref.shape[2]
    ones = ones_ref[...]
    y = y_ref[0, 0] + y_ref[1, 0]
    inv = 1.0 / A_HEAD
    mean = _segsum(y, ones) * inv
    yc = y - mean
    var = _segsum(yc * yc, ones) * inv
    yn = yc * lax.rsqrt(var + GN_EPS) * lg_ref[...] + lb_ref[...]
    bonus = _segsum(r_ref[0] * (kd_ref[0, 0] + kd_ref[1, 0]) * rk_ref[...], ones) * v_ref[0]
    o_ref[0, :, 0:aw] = ((yn + bonus) * g_ref[0]).astype(BF16)
    u = ub_ref[0]
    mu = jnp.mean(u, axis=-1, keepdims=True)
    uc = u - mu
    uv = jnp.mean(uc * uc, axis=-1, keepdims=True)
    un = uc * lax.rsqrt(uv + LN_EPS) * bg_ref[...] + bb_ref[...]
    o_ref[0, :, aw:] = (un * _sigmoid(un)).astype(BF16)


def _merge(y, r, v, kd, g, ub, r_k, lnx_g, lnx_b, lnb_g, lnb_b, n_ctx):
    b, t, aw = r.shape
    bw = ub.shape[2]
    tm = _tile_rows(n_ctx, t - n_ctx)
    one = pl.BlockSpec((1, tm, aw), lambda i, j: (i, j, 0))
    two = pl.BlockSpec((2, 1, tm, aw), lambda i, j: (0, i, j, 0))
    c2 = lambda shape: pl.BlockSpec(shape, lambda i, j: (0, 0))
    return pl.pallas_call(
        _merge_kernel, grid=(b, t // tm),
        in_specs=[two, one, one, two, one, pl.BlockSpec((1, tm, bw), lambda i, j: (i, j, 0)),
                  c2((1, aw)), c2((1, aw)), c2((1, aw)), c2((1, bw)), c2((1, bw)), c2((LANE, LANE))],
        out_specs=pl.BlockSpec((1, tm, aw + bw), lambda i, j: (i, j, 0)),
        out_shape=jax.ShapeDtypeStruct((b, t, aw + bw), BF16),
        compiler_params=_cparams(("parallel", "parallel")), name="rwkv_readout_conformer_norm",
    )(y, r, v, kd, g, ub, r_k.reshape(1, aw), lnx_g.reshape(1, aw), lnx_b.reshape(1, aw),
      lnb_g.reshape(1, bw), lnb_b.reshape(1, bw), _ones_blockdiag())


def _out_kernel(lhs_ref, x_ref, mod_ref, g_ref, w_ref, wr_ref, rb_ref, xo_ref, h_ref, eid_ref, gate_ref):
    m = mod_ref[0, 0]
    x = x_ref[0] + m[2:3] * jnp.dot(lhs_ref[0], w_ref[...], preferred_element_type=F32)
    xo_ref[0] = x
    h = _rms_mod(x, g_ref[...], m[3:4], m[4:5]).astype(BF16)
    h_ref[0] = h
    logits = lax.dot_general(wr_ref[...], h, (((1,), (1,)), ((), ())), preferred_element_type=F32)
    scores = _sigmoid(logits)
    sel = scores + rb_ref[...]
    e = EXPERTS_PER_GROUP
    best = None
    for gi in range(N_GROUPS):
        rows = [sel[gi * e + i:gi * e + i + 1] for i in range(e)]
        top2 = None
        for i in range(e):
            for k in range(i + 1, e):
                s = rows[i] + rows[k]
                top2 = s if top2 is None else jnp.maximum(top2, s)
        if best is None:
            best, best_v = jnp.zeros_like(top2, dtype=jnp.int32), top2
        else:
            upd = top2 > best_v
            best = jnp.where(upd, gi, best)
            best_v = jnp.where(upd, top2, best_v)

    def pick(a, i):
        out = a[i:i + 1]
        for gi in range(1, N_GROUPS):
            out = jnp.where(best == gi, a[gi * e + i:gi * e + i + 1], out)
        return out

    cs = [pick(sel, i) for i in range(e)]
    sc = [pick(scores, i) for i in range(e)]

    def argmax4(vals):
        idx, val = jnp.zeros_like(best), vals[0]
        for i in range(1, e):
            upd = vals[i] > val
            idx = jnp.where(upd, i, idx)
            val = jnp.where(upd, vals[i], val)
        return idx

    i1 = argmax4(cs)
    i2 = argmax4([jnp.where(i1 == i, -jnp.inf, cs[i]) for i in range(e)])

    def take(vals, idx):
        out = vals[0]
        for i in range(1, e):
            out = jnp.where(idx == i, vals[i], out)
        return out

    s1, s2 = take(sc, i1), take(sc, i2)
    tot = s1 + s2
    eid_ref[0] = jnp.concatenate([best * e + i1, best * e + i2], axis=0)
    gate_ref[0] = jnp.concatenate([s1 / tot, s2 / tot], axis=0)


def _out_proj(lhs, x, mod, gain, w, w_router, router_bias, n_ctx):
    b, t, d = x.shape
    tm = _tile_rows(n_ctx, t - n_ctx)
    nct = n_ctx // tm
    k = lhs.shape[2]
    ne = w_router.shape[1]
    tok = lambda i, j: (i, j, 0)
    c2 = lambda shape: pl.BlockSpec(shape, lambda i, j: (0, 0), pipeline_mode=pl.Buffered(1))
    return pl.pallas_call(
        _out_kernel, grid=(b, t // tm),
        in_specs=[pl.BlockSpec((1, tm, k), tok), pl.BlockSpec((1, tm, d), tok),
                  pl.BlockSpec((1, 1, 6, d), lambda i, j: (i, jnp.where(j < nct, 0, 1), 0, 0)),
                  c2((1, d)), c2((k, d)), c2((ne, d)), c2((ne, 1))],
        out_specs=[pl.BlockSpec((1, tm, d), tok), pl.BlockSpec((1, tm, d), tok),
                   pl.BlockSpec((1, 2, tm), lambda i, j: (i, 0, j)),
                   pl.BlockSpec((1, 2, tm), lambda i, j: (i, 0, j))],
        out_shape=[jax.ShapeDtypeStruct((b, t, d), F32), jax.ShapeDtypeStruct((b, t, d), BF16),
                   jax.ShapeDtypeStruct((b, 2, t), jnp.int32), jax.ShapeDtypeStruct((b, 2, t), F32)],
        compiler_params=_cparams(("parallel", "parallel")), name="out_proj_router",
    )(lhs, x, mod, gain.reshape(1, d), w, w_router.T.astype(BF16), router_bias.reshape(ne, 1))


def _expert_kernel(te_ref, nu_ref, x_ref, wg_ref, wu_ref, wd_ref, o_ref):
    i = pl.program_id(0)

    @pl.when(i < nu_ref[0])
    def _():
        x = x_ref[...]
        a = jnp.dot(x, wg_ref[0], preferred_element_type=F32)
        u = jnp.dot(x, wu_ref[0], preferred_element_type=F32)
        he = (a * _sigmoid(a) * u).astype(BF16)
        o_ref[...] = jnp.dot(he, wd_ref[0], preferred_element_type=F32).astype(BF16)

    @pl.when(i >= nu_ref[0])
    def _():
        o_ref[...] = jnp.zeros_like(o_ref)


def _experts(x_sorted, tile_expert, n_used, w_gate, w_up, w_down, tm):
    p, d = x_sorted.shape
    f = w_gate.shape[2]
    grid_spec = pltpu.PrefetchScalarGridSpec(
        num_scalar_prefetch=2, grid=(p // tm,),
        in_specs=[pl.BlockSpec((tm, d), lambda i, te, nu: (i, 0)),
                  pl.BlockSpec((1, d, f), lambda i, te, nu: (te[i], 0, 0)),
                  pl.BlockSpec((1, d, f), lambda i, te, nu: (te[i], 0, 0)),
                  pl.BlockSpec((1, f, d), lambda i, te, nu: (te[i], 0, 0))],
        out_specs=pl.BlockSpec((tm, d), lambda i, te, nu: (i, 0)))
    return pl.pallas_call(
        _expert_kernel, grid_spec=grid_spec,
        out_shape=jax.ShapeDtypeStruct((p, d), BF16),
        compiler_params=_cparams(("arbitrary",)), name="moe_experts",
    )(tile_expert, n_used, x_sorted, w_gate, w_up, w_down)


def _moe(h2, eid, gates, w_gate, w_up, w_down, tm):
    b, t, d = h2.shape
    n = b * t
    ne = w_gate.shape[0]
    e_flat = eid.transpose(0, 2, 1).reshape(n * 2)
    onehot = (e_flat[:, None] == jnp.arange(ne, dtype=jnp.int32)[None, :]).astype(jnp.int32)
    csum = jnp.cumsum(onehot, axis=0)
    counts = csum[-1]
    rank = jnp.sum((csum - 1) * onehot, axis=1)
    padded = ((counts + tm - 1) // tm) * tm
    ends = jnp.cumsum(padded)
    starts = ends - padded
    pos = jnp.sum(onehot * starts[None, :], axis=1) + rank
    n_tiles = (2 * n) // tm + ne
    tile_start = jnp.arange(n_tiles, dtype=jnp.int32) * tm
    n_used = (ends[-1] // tm).astype(jnp.int32)
    tile_expert = jnp.minimum(jnp.searchsorted(ends, tile_start, side="right"), ne - 1).astype(jnp.int32)
    last_used = tile_expert[jnp.maximum(n_used - 1, 0)]
    tile_expert = jnp.where(jnp.arange(n_tiles) < n_used, tile_expert, last_used)
    order = jnp.argsort(e_flat, stable=True).astype(jnp.int32)
    row_expert = jnp.repeat(tile_expert, tm)
    q = jnp.arange(n_tiles * tm, dtype=jnp.int32) - starts[row_expert]
    cidx = jnp.clip((jnp.cumsum(counts) - counts)[row_expert] + q, 0, 2 * n - 1)
    src = jnp.where(q < counts[row_expert], order[cidx] // 2, 0)
    x_sorted = jnp.take(h2.reshape(n, d), src, axis=0)
    y_sorted = _experts(x_sorted, tile_expert, n_used.reshape(1), w_gate, w_up, w_down, tm)
    pos2 = pos.reshape(n, 2)
    y1 = jnp.take(y_sorted, pos2[:, 0], axis=0).reshape(b, t, d)
    y2 = jnp.take(y_sorted, pos2[:, 1], axis=0).reshape(b, t, d)
    return y1, y2, gates.transpose(0, 2, 1)


def _final_kernel(x_ref, y1_ref, y2_ref, gt_ref, mprev_ref, g_ref, o_ref):
    x = _residual_in(x_ref, (y1_ref, y2_ref, gt_ref, mprev_ref))
    o_ref[0] = x * lax.rsqrt(jnp.mean(x * x, axis=-1, keepdims=True) + RMS_EPS) * g_ref[...]


def _final(x, delta, mod_prev, gain, n_ctx):
    b, t, d = x.shape
    tm = _tile_rows(n_ctx, t - n_ctx)
    nct = n_ctx // tm
    tok = pl.BlockSpec((1, tm, d), lambda i, j: (i, j + nct, 0))
    return pl.pallas_call(
        _final_kernel, grid=(b, (t - n_ctx) // tm),
        in_specs=[tok, tok, tok, pl.BlockSpec((1, tm, 2), lambda i, j: (i, j + nct, 0)),
                  pl.BlockSpec((1, 1, 6, d), lambda i, j: (i, 1, 0, 0)),
                  pl.BlockSpec((1, d), lambda i, j: (0, 0))],
        out_specs=pl.BlockSpec((1, tm, d), lambda i, j: (i, j, 0)),
        out_shape=jax.ShapeDtypeStruct((b, t - n_ctx, d), F32),
        compiler_params=_cparams(("parallel", "parallel")), name="final_norm",
    )(x, *delta, mod_prev, gain.reshape(1, d))


def kernel(x, c, ctx, c_ctx, norm1_g, norm2_g, w_mod, b_mod, w_in_e, mu_prev, mu_next, w0, w2, a0, a2, g2,
           k_k, k_a, r_k, lnx_g, lnx_b, conv_b_w, conv_b_bias, lnb_g, lnb_b, w_out_e, w_in_o, conv_c_w,
           w_out_o, w_router, router_bias, w_gate, w_up, w_down, final_g):
    b, s, d = x.shape
    n_ctx = ctx.shape[1]
    depth = w_mod.shape[0]
    tm = _tile_rows(n_ctx, s)
    aw = w0.shape[2]
    a_proj = mu_prev.shape[1]

    xa = jnp.concatenate([ctx, x], axis=1)
    rows = ((b + 1 + 7) // 8) * 8
    cv = jnp.zeros((rows, d), F32).at[:b].set(c).at[b].set(c_ctx)
    mods_all = _adaln(cv, w_mod, b_mod).reshape(depth, rows, 6, d)
    mods = [jnp.stack([jnp.broadcast_to(mods_all[i, b], (b, 6, d)), mods_all[i, :b]], axis=1)
            for i in range(depth)]

    delta = None
    for i in range(depth):
        j = i // 2
        mod_prev = mods[i - 1] if i else None
        if i % 2 == 0:
            xa, z = _in_proj("even", xa, delta, mod_prev, mods[i], norm1_g[i], w_in_e[j].astype(BF16), None,
                             n_ctx)
            r, v, kkn, g, ld, kd, icl = _prepare(z, mu_prev[j], mu_next[j], w0[j], w2[j], a0[j], a2[j], g2[j],
                                                 k_k[j], k_a[j], n_ctx)
            y = _scan(r, v, kkn, ld, kd, icl, n_ctx)
            ub = _conformer_conv(z, a_proj, conv_b_w[j], conv_b_bias[j], n_ctx)
            lhs = _merge(y, r, v, kd, g, ub, r_k[j].reshape(aw), lnx_g[j], lnx_b[j], lnb_g[j], lnb_b[j], n_ctx)
            w_out = w_out_e[j]
        else:
            xa, lhs = _in_proj("odd", xa, delta, mod_prev, mods[i], norm1_g[i], w_in_o[j].astype(BF16),
                               conv_c_w[j], n_ctx)
            w_out = w_out_o[j]
        xa, h2, eid, gates = _out_proj(lhs, xa, mods[i], norm2_g[i], w_out.astype(BF16), w_router, router_bias,
                                       n_ctx)
        delta = _moe(h2, eid, gates, w_gate[i].astype(BF16), w_up[i].astype(BF16), w_down[i].astype(BF16), tm)
    return _final(xa, delta, mods[depth - 1], final_g, n_ctx)
```

```python
import functools

import jax
import jax.numpy as jnp
from jax import lax
from jax.experimental import pallas as pl
from jax.experimental.pallas import tpu as pltpu

F32 = jnp.float32
BF16 = jnp.bfloat16

GRID_W = 64
A_HEAD = 64
DECAY_LORA = 64
ICL_LORA = 64
GATE_LORA = 128
N_EXPERTS = 16
N_GROUPS = 4
EXPERTS_PER_GROUP = N_EXPERTS // N_GROUPS
RMS_EPS = 1e-6
LN_EPS = 1e-5
GN_EPS = 64e-5

LANE = 128
CHUNK = 64
PAIR = 2 * A_HEAD
VMEM_LIMIT = 56 * 1024 * 1024


def _cparams(sem):
    return pltpu.CompilerParams(dimension_semantics=sem, vmem_limit_bytes=VMEM_LIMIT)


def _sigmoid(x):
    return 1.0 / (1.0 + jnp.exp(-x))


def _bdot(a, b):
    return jnp.dot(a.astype(BF16), b.astype(BF16), preferred_element_type=F32)


def _rms_mod(x, g, shift, scale):
    y = x * lax.rsqrt(jnp.mean(x * x, axis=-1, keepdims=True) + RMS_EPS)
    return (y * g) * (1.0 + scale) + shift


def _segsum(x, ones_bd):
    outs = []
    for c in range(x.shape[1] // LANE):
        xs = x[:, c * LANE:(c + 1) * LANE]
        hi = xs.astype(BF16)
        lo = (xs - hi.astype(F32)).astype(BF16)
        outs.append(jnp.dot(hi, ones_bd, preferred_element_type=F32)
                    + jnp.dot(lo, ones_bd, preferred_element_type=F32))
    return jnp.concatenate(outs, axis=1)


def _adaln_kernel(cv_ref, w_ref, b_ref, o_ref):
    cv = cv_ref[...]
    s = cv * _sigmoid(cv)
    o_ref[0] = _bdot(s, w_ref[0]) + b_ref[0]


def _adaln(cv, w_mod, b_mod):
    depth, d, n = w_mod.shape
    rows = cv.shape[0]
    tn = 1024
    return pl.pallas_call(
        _adaln_kernel,
        grid=(depth, n // tn),
        in_specs=[pl.BlockSpec((rows, d), lambda i, j: (0, 0)),
                  pl.BlockSpec((1, d, tn), lambda i, j: (i, 0, j)),
                  pl.BlockSpec((1, 1, tn), lambda i, j: (i, 0, j))],
        out_specs=pl.BlockSpec((1, rows, tn), lambda i, j: (i, 0, j)),
        out_shape=jax.ShapeDtypeStruct((depth, rows, n), F32),
        compiler_params=_cparams(("parallel", "parallel")),
        name="adaln",
    )(cv, w_mod, b_mod.reshape(depth, 1, n))


def _residual_in(x_ref, moe_refs):
    x = x_ref[0]
    if moe_refs is not None:
        y1_ref, y2_ref, gt_ref, mprev_ref = moe_refs
        gt = gt_ref[0]
        moe = gt[:, 0:1] * y1_ref[0].astype(F32) + gt[:, 1:2] * y2_ref[0].astype(F32)
        x = x + mprev_ref[0, 0][5:6] * moe
    return x


def _in_even_kernel(has_delta, nct, *refs):
    if has_delta:
        x = _residual_in(refs[0], refs[1:5])
        refs = refs[5:]
    else:
        x = jnp.where(pl.program_id(1) < nct, refs[0][0], refs[1][0])
        refs = refs[2:]
    mod_ref, g_ref, w_ref, xo_ref, z_ref = refs
    xo_ref[0] = x
    m = mod_ref[0, 0]
    h = _rms_mod(x, g_ref[...], m[0:1], m[1:2])
    z_ref[0] = jnp.dot(h.astype(BF16), w_ref[...], preferred_element_type=F32)


def _in_odd_kernel(n_ctx, *refs):
    x = _residual_in(refs[0], refs[1:5])
    mod_ref, g_ref, w_ref, cw_ref, xo_ref, v_ref = refs[5:]
    xo_ref[0] = x
    m = mod_ref[0, 0]
    h = _rms_mod(x, g_ref[...], m[0:1], m[1:2])
    z = jnp.dot(h.astype(BF16), w_ref[...], preferred_element_type=F32)
    tm, d = x.shape
    bg = z[:, 0:d]
    u = z[:, d:2 * d] * z[:, 2 * d:3 * d]
    pos = pl.program_id(1) * tm + lax.broadcasted_iota(jnp.int32, (tm, 1), 0)
    is_ctx = pos < n_ctx
    col = lax.rem(pos - n_ctx, GRID_W)
    prev_ok = jnp.where(is_ctx, pos, col) != 0
    next_ok = jnp.where(is_ctx, pos - (n_ctx - 1), col - (GRID_W - 1)) != 0
    u_prev = jnp.where(prev_ok, pltpu.roll(u, 1, axis=0), 0.0)
    u_next = jnp.where(next_ok, pltpu.roll(u, tm - 1, axis=0), 0.0)
    cw = cw_ref[...]
    conv = cw[0:1] * u_prev + cw[1:2] * u + cw[2:3] * u_next
    v_ref[0] = (bg * conv).astype(BF16)


def _tile_rows(n_ctx, seq):
    tm = 256
    while n_ctx % tm or seq % tm:
        tm //= 2
    assert tm % GRID_W == 0 and tm % CHUNK == 0
    return tm


def _in_proj(kind, x, delta, mod_prev, mod, gain, w, conv_w, n_ctx):
    has_delta = delta is not None
    if has_delta:
        b, t, d = x.shape
    else:
        b, t, d = x[0].shape[0], x[0].shape[1] + x[1].shape[1], x[0].shape[2]
    tm = _tile_rows(n_ctx, t - n_ctx)
    nct = n_ctx // tm
    n = w.shape[1]
    tok = pl.BlockSpec((1, tm, d), lambda i, j: (i, j, 0))
    modspec = pl.BlockSpec((1, 1, 6, d), lambda i, j: (i, jnp.where(j < nct, 0, 1), 0, 0))
    const2 = lambda shape: pl.BlockSpec(shape, lambda i, j: (0, 0), pipeline_mode=pl.Buffered(1))
    if has_delta:
        in_specs = [tok, tok, tok, pl.BlockSpec((1, tm, 2), lambda i, j: (i, j, 0)), modspec]
        args = [x, *delta, mod_prev]
    else:
        in_specs = [pl.BlockSpec((1, tm, d), lambda i, j: (i, jnp.minimum(j, nct - 1), 0)),
                    pl.BlockSpec((1, tm, d), lambda i, j: (i, jnp.maximum(j - nct, 0), 0))]
        args = [*x]
    in_specs += [modspec, const2((1, d)), const2((d, n))]
    args += [mod, gain.reshape(1, d), w]
    if kind == "even":
        body = functools.partial(_in_even_kernel, has_delta, nct)
        out_specs = [tok, pl.BlockSpec((1, tm, n), lambda i, j: (i, j, 0))]
        out_shape = [jax.ShapeDtypeStruct((b, t, d), F32), jax.ShapeDtypeStruct((b, t, n), F32)]
    else:
        assert has_delta
        body = functools.partial(_in_odd_kernel, n_ctx)
        in_specs.append(const2((3, d)))
        args.append(conv_w)
        out_specs = [tok, tok]
        out_shape = [jax.ShapeDtypeStruct((b, t, d), F32), jax.ShapeDtypeStruct((b, t, d), BF16)]
    return pl.pallas_call(
        body, grid=(b, t // tm), in_specs=in_specs, out_specs=out_specs, out_shape=out_shape,
        compiler_params=_cparams(("parallel", "parallel")), name="in_proj_" + kind,
    )(*args)


def _prepare_kernel(n_ctx, aw, zc_ref, zp_ref, zn_ref, mup_ref, mun_ref, w0_ref, w2_ref, a0_ref, a2_ref,
                    g2_ref, kk_ref, ka_ref, ones_ref,
                    r_ref, v_ref, kkn_ref, g_ref, ld_ref, kd_ref, icl_ref):
    z = zc_ref[0]
    tm = z.shape[0]
    j = pl.program_id(1)
    nct = n_ctx // tm
    first = jnp.logical_or(j == 0, j == nct)
    last = jnp.logical_or(j == nct - 1, j == pl.num_programs(1) - 1)
    row = lax.broadcasted_iota(jnp.int32, (tm, 1), 0)
    halo_p = jnp.where(first, 0.0, zp_ref[0][7:8])
    halo_n = jnp.where(last, 0.0, zn_ref[0][0:1])
    z_prev = jnp.where(row == 0, halo_p, pltpu.roll(z, 1, axis=0))
    z_next = jnp.where(row == tm - 1, halo_n, pltpu.roll(z, tm - 1, axis=0))
    z = z + mup_ref[...] * (z_prev - z) + mun_ref[...] * (z_next - z)
    r = z[:, 0:aw]
    k = z[:, aw:2 * aw]
    v = z[:, 2 * aw:3 * aw]
    o = 3 * aw
    dw = jnp.tanh(z[:, o:o + 2 * DECAY_LORA])
    da = z[:, o + 2 * DECAY_LORA:o + 2 * DECAY_LORA + 2 * ICL_LORA]
    dg = _sigmoid(z[:, o + 2 * DECAY_LORA + 2 * ICL_LORA:o + 2 * DECAY_LORA + 2 * ICL_LORA + GATE_LORA])
    r_ref[0] = r.astype(BF16)
    v_ref[0] = v.astype(BF16)
    g_ref[0] = _bdot(dg, g2_ref[...]).astype(BF16)
    kk = k * kk_ref[...]
    nrm = jnp.sqrt(_segsum(kk * kk, ones_ref[...]))
    kkn_ref[0] = (kk / jnp.maximum(nrm, 1e-12)).astype(BF16)
    for dr in range(2):
        x = -(w0_ref[dr:dr + 1] + _bdot(dw, w2_ref[dr]))
        softplus = jnp.maximum(x, 0.0) + jnp.log(1.0 + jnp.exp(-jnp.abs(x)))
        ld_ref[dr, 0] = -jnp.exp(-softplus - 0.5)
        icl = _sigmoid(a0_ref[dr:dr + 1] + _bdot(da, a2_ref[dr]))
        icl_ref[dr, 0] = icl.astype(BF16)
        kd_ref[dr, 0] = (k * (1.0 + (icl - 1.0) * ka_ref[...])).astype(BF16)


def _pad_lora(w):
    z = jnp.zeros_like(w[0])
    return jnp.stack([jnp.concatenate([w[0], z], 0), jnp.concatenate([z, w[1]], 0)]).astype(BF16)


def _ones_blockdiag():
    i = jnp.arange(LANE) // A_HEAD
    return (i[:, None] == i[None, :]).astype(BF16)


def _prepare(z, mu_prev, mu_next, w0, w2, a0, a2, g2, k_k, k_a, n_ctx):
    b, t, _ = z.shape
    aw = w0.shape[1]
    ap = mu_prev.shape[0]
    tm = _tile_rows(n_ctx, t - n_ctx)
    hb = tm // 8
    nhb = t // 8
    tok = lambda i, j: (i, j, 0)
    c2 = lambda shape: pl.BlockSpec(shape, lambda i, j: (0,) * len(shape))
    in_specs = [pl.BlockSpec((1, tm, ap), tok),
                pl.BlockSpec((1, 8, ap), lambda i, j: (i, jnp.maximum(j * hb - 1, 0), 0)),
                pl.BlockSpec((1, 8, ap), lambda i, j: (i, jnp.minimum((j + 1) * hb, nhb - 1), 0)),
                c2((1, ap)), c2((1, ap)), c2((2, aw)), c2((2, 2 * DECAY_LORA, aw)), c2((2, aw)),
                c2((2, 2 * ICL_LORA, aw)), c2((GATE_LORA, aw)), c2((1, aw)), c2((1, aw)), c2((LANE, LANE))]
    one = pl.BlockSpec((1, tm, aw), tok)
    two = pl.BlockSpec((2, 1, tm, aw), lambda i, j: (0, i, j, 0))
    s1 = jax.ShapeDtypeStruct((b, t, aw), BF16)
    s2 = jax.ShapeDtypeStruct((2, b, t, aw), BF16)
    return pl.pallas_call(
        functools.partial(_prepare_kernel, n_ctx, aw),
        grid=(b, t // tm), in_specs=in_specs,
        out_specs=[one, one, one, one, two, two, two],
        out_shape=[s1, s1, s1, s1, jax.ShapeDtypeStruct((2, b, t, aw), F32), s2, s2],
        compiler_params=_cparams(("parallel", "parallel")), name="rwkv_prepare",
    )(z, z, z, mu_prev.reshape(1, ap), mu_next.reshape(1, ap), w0, _pad_lora(w2), a0, _pad_lora(a2),
      g2.astype(BF16), k_k.reshape(1, aw), k_a.reshape(1, aw), _ones_blockdiag())


def _scan_kernel(n_ctx_chunks, r_ref, v_ref, kk_ref, ld_ref, kd_ref, icl_ref, y_ref, z_scr):
    dr = pl.program_id(1)
    step = pl.program_id(2)
    fwd = dr == 0
    c = CHUNK

    @pl.when(step == 0)
    def _():
        z_scr[...] = jnp.zeros_like(z_scr)

    ti = lax.broadcasted_iota(jnp.int32, (c, c), 0)
    tj = lax.broadcasted_iota(jnp.int32, (c, c), 1)
    tri = jnp.where(jnp.where(fwd, tj, ti) <= jnp.where(fwd, ti, tj), 1.0, 0.0).astype(BF16)
    ld = ld_ref[0, 0]
    ld_hi = ld.astype(BF16)
    ld_lo = (ld - ld_hi.astype(F32)).astype(BF16)
    linc = (jnp.dot(tri, ld_hi, preferred_element_type=F32)
            + jnp.dot(tri, ld_lo, preferred_element_type=F32))
    lexc = linc - ld
    ltot = jnp.where(fwd, linc[c - 1:c], linc[0:1])
    kk = kk_ref[0].astype(F32)
    bvec = kk * icl_ref[0, 0].astype(F32)
    kd = kd_ref[0, 0].astype(F32)
    e_neg = jnp.exp(-linc)
    e_tot = jnp.exp(ltot - linc)
    a_t = -kk * jnp.exp(lexc)
    r_t = r_ref[0].astype(F32) * jnp.exp(linc)
    b_t = bvec * e_neg
    k_t = kd * e_neg
    b_h = bvec * e_tot
    k_h = kd * e_tot
    p_c = jnp.exp(ltot)
    vv = v_ref[0].astype(F32)

    l64 = lax.broadcasted_iota(jnp.int32, (c, PAIR), 1)
    r64 = lax.broadcasted_iota(jnp.int32, (c, PAIR), 0)
    lane_a = l64 < A_HEAD
    li = jnp.where(lane_a, l64, l64 - A_HEAD)
    before = jnp.where(fwd, li, r64) < jnp.where(fwd, r64, li)
    before_eq = jnp.logical_or(before, li == r64)
    eye_pl = jnp.where(li == r64, 1.0, 0.0)
    r128 = lax.broadcasted_iota(jnp.int32, (PAIR, PAIR), 0)
    l128 = lax.broadcasted_iota(jnp.int32, (PAIR, PAIR), 1)
    diag_blk = (r128 < A_HEAD) == (l128 < A_HEAD)
    eye128 = r128 == l128
    lane_a2 = l128 < A_HEAD

    def bd(x):
        x = x.astype(F32)
        return jnp.where(diag_blk, jnp.concatenate([x, x], axis=0), 0.0).astype(BF16)

    def abd(x):
        x = x.astype(F32)
        return jnp.where(diag_blk, 0.0, jnp.concatenate([x, x], axis=0)).astype(BF16)

    nt = (((1,), (1,)), ((), ()))
    pairs = range(r_ref.shape[2] // PAIR)
    sls = [slice(p * PAIR, (p + 1) * PAIR) for p in pairs]
    mm = lambda a, b: jnp.dot(a.astype(BF16), b, preferred_element_type=F32)
    g1, g2 = [], []
    for sl in sls:
        ar = jnp.concatenate([a_t[:, sl], r_t[:, sl]], axis=0)
        bk = jnp.concatenate([b_t[:, sl], k_t[:, sl]], axis=0).astype(BF16)
        kb = jnp.concatenate([k_t[:, sl], b_t[:, sl]], axis=0).astype(BF16)
        g1.append(lax.dot_general(jnp.where(lane_a2, ar, 0.0).astype(BF16), bk, nt, preferred_element_type=F32))
        g2.append(lax.dot_general(jnp.where(lane_a2, 0.0, ar).astype(BF16), kb, nt, preferred_element_type=F32))
    sel_a = jnp.logical_and(before, lane_a)
    sel_a_eq = jnp.logical_and(before_eq, lane_a)
    ab = [jnp.where(sel_a, x[:c], jnp.where(before, y[:c], 0.0)) for x, y in zip(g1, g2)]
    rb = [jnp.where(sel_a_eq, x[c:], jnp.where(before_eq, y[c:], 0.0)) for x, y in zip(g1, g2)]
    ak_s = [jnp.where(sel_a, y[:c], jnp.where(before, x[:c], 0.0)) for x, y in zip(g1, g2)]
    rk_s = [jnp.where(sel_a_eq, y[c:], jnp.where(before_eq, x[c:], 0.0)) for x, y in zip(g1, g2)]
    akv_rkv = [mm(jnp.concatenate([a, b], axis=0), abd(vv[:, sl])) for a, b, sl in zip(ak_s, rk_s, sls)]
    s = [eye_pl + x for x in ab]
    q = [mm(x, bd(x)) for x in ab]
    n = 2
    while n < c:
        if 2 * n >= c:
            s = [si + mm(qi, bd(si)) for qi, si in zip(q, s)]
        else:
            out = [mm(qi, jnp.concatenate([bd(qi), bd(si)], axis=1)) for qi, si in zip(q, s)]
            s = [si + o[:, PAIR:] for si, o in zip(s, out)]
            q = [o[:, :PAIR] for o in out]
        n *= 2
    ah_w2 = [mm(si, jnp.concatenate([bd(a_t[:, sl]), bd(x[:c])], axis=1)) for si, sl, x in zip(s, sls, akv_rkv)]
    rb_o = [mm(x, jnp.concatenate([bd(o[:, :PAIR]), bd(o[:, PAIR:])], axis=1)) for x, o in zip(rb, ah_w2)]
    pg = []
    for sl, o in zip(sls, ah_w2):
        v_p = vv[:, sl]
        bkh_t = jnp.concatenate([b_h[:, sl], k_h[:, sl]], axis=0).T
        rhs = jnp.concatenate([o, jnp.concatenate([jnp.zeros_like(v_p), v_p], axis=1)], axis=0)
        pg.append(mm(bkh_t, rhs.astype(BF16)))
    ys, zs = [], []
    for p, sl in zip(pairs, sls):
        rbar = r_t[:, sl] + rb_o[p][:, :PAIR]
        y0 = rb_o[p][:, PAIR:] + akv_rkv[p][c:]
        pc_row = jnp.broadcast_to(p_c[:, sl], (PAIR, PAIR))
        phi = jnp.where(diag_blk, pg[p][:, :PAIR], 0.0) + jnp.where(eye128, pc_row, 0.0)
        gam = jnp.where(diag_blk, pg[p][:, PAIR:], 0.0)
        yz = mm(jnp.concatenate([rbar, phi], axis=0), z_scr[p].astype(BF16))
        ys.append(yz[:c] + y0)
        zs.append(yz[c:] + gam)
    y_ref[0, 0] = jnp.concatenate(ys, axis=1)
    for p in pairs:
        z_scr[p] = zs[p]


def _scan(r, v, kkn, ld, kd, icl, n_ctx):
    b, t, aw = r.shape
    nch = t // CHUNK
    ncc = n_ctx // CHUNK

    def chunk_of(d, i):
        rev = jnp.where(i < ncc, ncc - 1 - i, nch - 1 - (i - ncc))
        return jnp.where(d == 0, i, rev)

    one = pl.BlockSpec((1, CHUNK, aw), lambda bi, d, i: (bi, chunk_of(d, i), 0))
    two = pl.BlockSpec((1, 1, CHUNK, aw), lambda bi, d, i: (d, bi, chunk_of(d, i), 0))
    return pl.pallas_call(
        functools.partial(_scan_kernel, ncc),
        grid=(b, 2, nch),
        in_specs=[one, one, one, two, two, two],
        out_specs=two,
        out_shape=jax.ShapeDtypeStruct((2, b, t, aw), F32),
        scratch_shapes=[pltpu.VMEM((aw // PAIR, PAIR, PAIR), F32)],
        compiler_params=_cparams(("parallel", "parallel", "arbitrary")), name="wkv7_scan",
    )(r, v, kkn, ld, kd, icl)


def _conv_kernel(n_ctx, width, u_ref, gate_ref, w_ref, bias_ref, o_ref, lat_scr, ctx_scr):
    half = (width - 1) // 2
    t = u_ref.shape[1]
    seq = t - n_ctx
    padl = half * GRID_W
    padc = 16
    lat_scr[0:padl] = jnp.zeros((padl, LANE), F32)
    lat_scr[padl + seq:] = jnp.zeros((padl, LANE), F32)
    lat_scr[padl:padl + seq] = u_ref[0, n_ctx:] * _sigmoid(gate_ref[0, n_ctx:])
    ctx_scr[0:padc] = jnp.zeros((padc, LANE), F32)
    ctx_scr[padc + n_ctx:] = jnp.zeros((padc, LANE), F32)
    ctx_scr[padc:padc + n_ctx] = u_ref[0, 0:n_ctx] * _sigmoid(gate_ref[0, 0:n_ctx])
    bias = bias_ref[...]
    acc = jnp.zeros((n_ctx, LANE), F32)
    for j in range(width):
        acc = acc + w_ref[j:j + 1] * ctx_scr[padc - half + j:padc - half + j + n_ctx]
    o_ref[0, 0:n_ctx] = acc + bias

    def body(rc, carry):
        acc = jnp.zeros((GRID_W, LANE), F32)
        for j in range(width):
            acc = acc + w_ref[j:j + 1] * lat_scr[pl.ds(pl.multiple_of((rc + j) * GRID_W, GRID_W), GRID_W)]
        o_ref[0, pl.ds(pl.multiple_of(n_ctx + rc * GRID_W, GRID_W), GRID_W)] = acc + bias
        return carry

    lax.fori_loop(0, seq // GRID_W, body, 0)


def _conformer_conv(z, col0, conv_w, conv_bias, n_ctx):
    b, t, _ = z.shape
    width, bw = conv_w.shape
    half = (width - 1) // 2
    nb = bw // LANE
    c0 = col0 // LANE
    return pl.pallas_call(
        functools.partial(_conv_kernel, n_ctx, width),
        grid=(b, nb),
        in_specs=[pl.BlockSpec((1, t, LANE), lambda i, j: (i, 0, c0 + j)),
                  pl.BlockSpec((1, t, LANE), lambda i, j: (i, 0, c0 + nb + j)),
                  pl.BlockSpec((width, LANE), lambda i, j: (0, j)),
                  pl.BlockSpec((1, LANE), lambda i, j: (0, j))],
        out_specs=pl.BlockSpec((1, t, LANE), lambda i, j: (i, 0, j)),
        out_shape=jax.ShapeDtypeStruct((b, t, bw), F32),
        scratch_shapes=[pltpu.VMEM((t - n_ctx + 2 * half * GRID_W, LANE), F32),
                        pltpu.VMEM((n_ctx + 32, LANE), F32)],
        compiler_params=_cparams(("parallel", "parallel")), name="conformer_conv",
    )(z, z, conv_w, conv_bias.reshape(1, bw))


def _merge_kernel(y_ref, r_ref, v_ref, kd_ref, g_ref, ub_ref, rk_ref, lg_ref, lb_ref, bg_ref, bb_ref,
                  ones_ref, o_ref):
    aw = r_ref.shape[2]
    ones = ones_ref[...]
    y = y_ref[0, 0] + y_ref[1, 0]
    inv = 1.0 / A_HEAD
    mean = _segsum(y, ones) * inv
    yc = y - mean
    var = _segsum(yc * yc, ones) * inv
    yn = yc * lax.rsqrt(var + GN_EPS) * lg_ref[...] + lb_ref[...]
    kd_sum = kd_ref[0, 0].astype(F32) + kd_ref[1, 0].astype(F32)
    bonus = _segsum(r_ref[0].astype(F32) * kd_sum * rk_ref[...], ones) * v_ref[0].astype(F32)
    o_ref[0, :, 0:aw] = ((yn + bonus) * g_ref[0].astype(F32)).astype(BF16)
    u = ub_ref[0]
    mu = jnp.mean(u, axis=-1, keepdims=True)
    uc = u - mu
    uv = jnp.mean(uc * uc, axis=-1, keepdims=True)
    un = uc * lax.rsqrt(uv + LN_EPS) * bg_ref[...] + bb_ref[...]
    o_ref[0, :, aw:] = (un * _sigmoid(un)).astype(BF16)


def _merge(y, r, v, kd, g, ub, r_k, lnx_g, lnx_b, lnb_g, lnb_b, n_ctx):
    b, t, aw = r.shape
    bw = ub.shape[2]
    tm = _tile_rows(n_ctx, t - n_ctx)
    one = pl.BlockSpec((1, tm, aw), lambda i, j: (i, j, 0))
    two = pl.BlockSpec((2, 1, tm, aw), lambda i, j: (0, i, j, 0))
    c2 = lambda shape: pl.BlockSpec(shape, lambda i, j: (0, 0))
    return pl.pallas_call(
        _merge_kernel, grid=(b, t // tm),
        in_specs=[two, one, one, two, one, pl.BlockSpec((1, tm, bw), lambda i, j: (i, j, 0)),
                  c2((1, aw)), c2((1, aw)), c2((1, aw)), c2((1, bw)), c2((1, bw)), c2((LANE, LANE))],
        out_specs=pl.BlockSpec((1, tm, aw + bw), lambda i, j: (i, j, 0)),
        out_shape=jax.ShapeDtypeStruct((b, t, aw + bw), BF16),
        compiler_params=_cparams(("parallel", "parallel")), name="rwkv_readout_conformer_norm",
    )(y, r, v, kd, g, ub, r_k.reshape(1, aw), lnx_g.reshape(1, aw), lnx_b.reshape(1, aw),
      lnb_g.reshape(1, bw), lnb_b.reshape(1, bw), _ones_blockdiag())


def _out_kernel(lhs_ref, x_ref, mod_ref, g_ref, w_ref, wr_ref, rb_ref, xo_ref, h_ref, eid_ref, gate_ref):
    m = mod_ref[0, 0]
    x = x_ref[0] + m[2:3] * jnp.dot(lhs_ref[0], w_ref[...], preferred_element_type=F32)
    xo_ref[0] = x
    h = _rms_mod(x, g_ref[...], m[3:4], m[4:5]).astype(BF16)
    h_ref[0] = h
    logits = lax.dot_general(wr_ref[...], h, (((1,), (1,)), ((), ())), preferred_element_type=F32)
    scores = _sigmoid(logits)
    sel = scores + rb_ref[...]
    e = EXPERTS_PER_GROUP
    best = None
    for gi in range(N_GROUPS):
        rows = [sel[gi * e + i:gi * e + i + 1] for i in range(e)]
        top2 = None
        for i in range(e):
            for k in range(i + 1, e):
                s = rows[i] + rows[k]
                top2 = s if top2 is None else jnp.maximum(top2, s)
        if best is None:
            best, best_v = jnp.zeros_like(top2, dtype=jnp.int32), top2
        else:
            upd = top2 > best_v
            best = jnp.where(upd, gi, best)
            best_v = jnp.where(upd, top2, best_v)

    def pick(a, i):
        out = a[i:i + 1]
        for gi in range(1, N_GROUPS):
            out = jnp.where(best == gi, a[gi * e + i:gi * e + i + 1], out)
        return out

    cs = [pick(sel, i) for i in range(e)]
    sc = [pick(scores, i) for i in range(e)]

    def argmax4(vals):
        idx, val = jnp.zeros_like(best), vals[0]
        for i in range(1, e):
            upd = vals[i] > val
            idx = jnp.where(upd, i, idx)
            val = jnp.where(upd, vals[i], val)
        return idx

    i1 = argmax4(cs)
    i2 = argmax4([jnp.where(i1 == i, -jnp.inf, cs[i]) for i in range(e)])

    def take(vals, idx):
        out = vals[0]
        for i in range(1, e):
            out = jnp.where(idx == i, vals[i], out)
        return out

    s1, s2 = take(sc, i1), take(sc, i2)
    tot = s1 + s2
    eid_ref[0] = jnp.concatenate([best * e + i1, best * e + i2], axis=0)
    gate_ref[0] = jnp.concatenate([s1 / tot, s2 / tot], axis=0)


def _out_proj(lhs, x, mod, gain, w, w_router, router_bias, n_ctx):
    b, t, d = x.shape
    tm = _tile_rows(n_ctx, t - n_ctx)
    nct = n_ctx // tm
    k = lhs.shape[2]
    ne = w_router.shape[1]
    tok = lambda i, j: (i, j, 0)
    c2 = lambda shape: pl.BlockSpec(shape, lambda i, j: (0, 0), pipeline_mode=pl.Buffered(1))
    return pl.pallas_call(
        _out_kernel, grid=(b, t // tm),
        in_specs=[pl.BlockSpec((1, tm, k), tok), pl.BlockSpec((1, tm, d), tok),
                  pl.BlockSpec((1, 1, 6, d), lambda i, j: (i, jnp.where(j < nct, 0, 1), 0, 0)),
                  c2((1, d)), c2((k, d)), c2((ne, d)), c2((ne, 1))],
        out_specs=[pl.BlockSpec((1, tm, d), tok), pl.BlockSpec((1, tm, d), tok),
                   pl.BlockSpec((1, 2, tm), lambda i, j: (i, 0, j)),
                   pl.BlockSpec((1, 2, tm), lambda i, j: (i, 0, j))],
        out_shape=[jax.ShapeDtypeStruct((b, t, d), F32), jax.ShapeDtypeStruct((b, t, d), BF16),
                   jax.ShapeDtypeStruct((b, 2, t), jnp.int32), jax.ShapeDtypeStruct((b, 2, t), F32)],
        compiler_params=_cparams(("parallel", "parallel")), name="out_proj_router",
    )(lhs, x, mod, gain.reshape(1, d), w, w_router.T.astype(BF16), router_bias.reshape(ne, 1))


def _expert_kernel(it_ref, ie_ref, fl_ref, lo_ref, hi_ref, ni_ref, x_ref, wg_ref, wu_ref, wd_ref, o_ref,
                   wg_s, wu_s, wd_s):
    j = pl.program_id(0)
    tm = x_ref.shape[0]

    @pl.when(j < ni_ref[0])
    def _():
        @pl.when((fl_ref[j] & 1) != 0)
        def _():
            wg_s[...] = wg_ref[0, 0].astype(BF16)
            wu_s[...] = wu_ref[0, 0].astype(BF16)
            wd_s[...] = wd_ref[0, 0].astype(BF16)

        x = x_ref[...]
        a = jnp.dot(x, wg_s[...], preferred_element_type=F32)
        u = jnp.dot(x, wu_s[...], preferred_element_type=F32)
        he = (a * _sigmoid(a) * u).astype(BF16)
        y = jnp.dot(he, wd_s[...], preferred_element_type=F32).astype(BF16)
        row = it_ref[j] * tm + lax.broadcasted_iota(jnp.int32, (tm, 1), 0)
        mine = jnp.logical_and(row >= lo_ref[j], row < hi_ref[j])

        @pl.when((fl_ref[j] & 2) != 0)
        def _():
            o_ref[...] = jnp.where(mine, y, jnp.zeros_like(y))

        @pl.when((fl_ref[j] & 2) == 0)
        def _():
            o_ref[...] = jnp.where(mine, y, o_ref[...])


def _experts(x_sorted, items, layer, w_gate, w_up, w_down, tm):
    p, d = x_sorted.shape
    f = w_gate.shape[3]
    n_items = items[0].shape[0]
    wspec = lambda shape: pl.BlockSpec((1, 1) + shape, lambda j, it, ie, *_: (layer, ie[j], 0, 0),
                                       pipeline_mode=pl.Buffered(1))
    grid_spec = pltpu.PrefetchScalarGridSpec(
        num_scalar_prefetch=6, grid=(n_items,),
        in_specs=[pl.BlockSpec((tm, d), lambda j, it, *_: (it[j], 0)), wspec((d, f)), wspec((d, f)), wspec((f, d))],
        out_specs=pl.BlockSpec((tm, d), lambda j, it, *_: (it[j], 0)),
        scratch_shapes=[pltpu.VMEM((d, f), BF16), pltpu.VMEM((d, f), BF16), pltpu.VMEM((f, d), BF16)])
    return pl.pallas_call(
        _expert_kernel, grid_spec=grid_spec,
        out_shape=jax.ShapeDtypeStruct((p, d), BF16),
        compiler_params=_cparams(("arbitrary",)), name="moe_experts",
    )(*items, x_sorted, w_gate, w_up, w_down)


def _moe(h2, eid, gates, layer, w_gate, w_up, w_down, tm):
    b, t, d = h2.shape
    n = b * t
    ne = w_gate.shape[1]
    e_flat = eid.transpose(0, 2, 1).reshape(n * 2)
    slot = jnp.arange(2 * n, dtype=jnp.int32)
    sorted_e, order = lax.sort((e_flat, slot), num_keys=1, is_stable=True)
    _, pos = lax.sort((order, slot), num_keys=1)
    bounds = jnp.searchsorted(sorted_e, jnp.arange(ne + 1, dtype=jnp.int32), side="left").astype(jnp.int32)
    lo_t = bounds[:-1] // tm
    n_e = jnp.where(bounds[1:] > bounds[:-1], (bounds[1:] - 1) // tm - lo_t + 1, 0)
    item_end = jnp.cumsum(n_e)
    item_start = item_end - n_e
    n_items = item_end[-1]
    max_items = (2 * n) // tm + ne - 1
    jj = jnp.minimum(jnp.arange(max_items, dtype=jnp.int32), n_items - 1)
    ie = jnp.minimum(jnp.searchsorted(item_end, jj, side="right"), ne - 1).astype(jnp.int32)
    onehot = ie[:, None] == jnp.arange(ne, dtype=jnp.int32)[None, :]
    pick = lambda tab: jnp.sum(jnp.where(onehot, tab[None, :], 0), axis=1).astype(jnp.int32)
    it = pick(lo_t) + jj - pick(item_start)
    first = jnp.arange(max_items) == 0
    new_e = jnp.logical_or(first, ie != jnp.roll(ie, 1))
    new_t = jnp.logical_or(first, it != jnp.roll(it, 1))
    flags = new_e.astype(jnp.int32) + 2 * new_t.astype(jnp.int32)
    items = (it, ie, flags, pick(bounds[:-1]), pick(bounds[1:]), n_items.reshape(1).astype(jnp.int32))
    x_sorted = jnp.take(h2.reshape(n, d), order // 2, axis=0)
    y_sorted = _experts(x_sorted, items, layer, w_gate, w_up, w_down, tm)
    pos2 = pos.reshape(n, 2)
    y1 = jnp.take(y_sorted, pos2[:, 0], axis=0).reshape(b, t, d)
    y2 = jnp.take(y_sorted, pos2[:, 1], axis=0).reshape(b, t, d)
    return y1, y2, gates.transpose(0, 2, 1)


def _final_kernel(x_ref, y1_ref, y2_ref, gt_ref, mprev_ref, g_ref, o_ref):
    x = _residual_in(x_ref, (y1_ref, y2_ref, gt_ref, mprev_ref))
    o_ref[0] = x * lax.rsqrt(jnp.mean(x * x, axis=-1, keepdims=True) + RMS_EPS) * g_ref[...]


def _final(x, delta, mod_prev, gain, n_ctx):
    b, t, d = x.shape
    tm = _tile_rows(n_ctx, t - n_ctx)
    nct = n_ctx // tm
    tok = pl.BlockSpec((1, tm, d), lambda i, j: (i, j + nct, 0))
    return pl.pallas_call(
        _final_kernel, grid=(b, (t - n_ctx) // tm),
        in_specs=[tok, tok, tok, pl.BlockSpec((1, tm, 2), lambda i, j: (i, j + nct, 0)),
                  pl.BlockSpec((1, 1, 6, d), lambda i, j: (i, 1, 0, 0)),
                  pl.BlockSpec((1, d), lambda i, j: (0, 0))],
        out_specs=pl.BlockSpec((1, tm, d), lambda i, j: (i, j, 0)),
        out_shape=jax.ShapeDtypeStruct((b, t - n_ctx, d), F32),
        compiler_params=_cparams(("parallel", "parallel")), name="final_norm",
    )(x, *delta, mod_prev, gain.reshape(1, d))


def kernel(x, c, ctx, c_ctx, norm1_g, norm2_g, w_mod, b_mod, w_in_e, mu_prev, mu_next, w0, w2, a0, a2, g2,
           k_k, k_a, r_k, lnx_g, lnx_b, conv_b_w, conv_b_bias, lnb_g, lnb_b, w_out_e, w_in_o, conv_c_w,
           w_out_o, w_router, router_bias, w_gate, w_up, w_down, final_g):
    b, s, d = x.shape
    n_ctx = ctx.shape[1]
    depth = w_mod.shape[0]
    tm = _tile_rows(n_ctx, s)
    aw = w0.shape[2]
    a_proj = mu_prev.shape[1]

    xa = (ctx, x)
    rows = ((b + 1 + 7) // 8) * 8
    cv = jnp.zeros((rows, d), F32).at[:b].set(c).at[b].set(c_ctx)
    mods_all = _adaln(cv, w_mod, b_mod).reshape(depth, rows, 6, d)
    mods = [jnp.stack([jnp.broadcast_to(mods_all[i, b], (b, 6, d)), mods_all[i, :b]], axis=1)
            for i in range(depth)]

    delta = None
    for i in range(depth):
        j = i // 2
        mod_prev = mods[i - 1] if i else None
        if i % 2 == 0:
            xa, z = _in_proj("even", xa, delta, mod_prev, mods[i], norm1_g[i], w_in_e[j].astype(BF16), None,
                             n_ctx)
            r, v, kkn, g, ld, kd, icl = _prepare(z, mu_prev[j], mu_next[j], w0[j], w2[j], a0[j], a2[j], g2[j],
                                                 k_k[j], k_a[j], n_ctx)
            y = _scan(r, v, kkn, ld, kd, icl, n_ctx)
            ub = _conformer_conv(z, a_proj, conv_b_w[j], conv_b_bias[j], n_ctx)
            lhs = _merge(y, r, v, kd, g, ub, r_k[j].reshape(aw), lnx_g[j], lnx_b[j], lnb_g[j], lnb_b[j], n_ctx)
            w_out = w_out_e[j]
        else:
            xa, lhs = _in_proj("odd", xa, delta, mod_prev, mods[i], norm1_g[i], w_in_o[j].astype(BF16),
                               conv_c_w[j], n_ctx)
            w_out = w_out_o[j]
        xa, h2, eid, gates = _out_proj(lhs, xa, mods[i], norm2_g[i], w_out.astype(BF16), w_router, router_bias,
                                       n_ctx)
        delta = _moe(h2, eid, gates, i, w_gate, w_up, w_down, tm)
    return _final(xa, delta, mods[depth - 1], final_g, n_ctx)
```

```python
import functools

import jax
import jax.numpy as jnp
from jax import lax
from jax.experimental import pallas as pl
from jax.experimental.pallas import tpu as pltpu

F32 = jnp.float32
BF16 = jnp.bfloat16

GRID_W = 64
A_HEAD = 64
DECAY_LORA = 64
ICL_LORA = 64
GATE_LORA = 128
N_EXPERTS = 16
N_GROUPS = 4
EXPERTS_PER_GROUP = N_EXPERTS // N_GROUPS
RMS_EPS = 1e-6
LN_EPS = 1e-5
GN_EPS = 64e-5

LANE = 128
CHUNK = 64
PAIR = 2 * A_HEAD
VMEM_LIMIT = 56 * 1024 * 1024


def _cparams(sem):
    return pltpu.CompilerParams(dimension_semantics=sem, vmem_limit_bytes=VMEM_LIMIT)


def _sigmoid(x):
    return 1.0 / (1.0 + jnp.exp(-x))


def _bdot(a, b):
    return jnp.dot(a.astype(BF16), b.astype(BF16), preferred_element_type=F32)


def _rms_mod(x, g, shift, scale):
    y = x * lax.rsqrt(jnp.mean(x * x, axis=-1, keepdims=True) + RMS_EPS)
    return (y * g) * (1.0 + scale) + shift


def _segsum(x, ones_bd):
    outs = []
    for c in range(x.shape[1] // LANE):
        xs = x[:, c * LANE:(c + 1) * LANE]
        hi = xs.astype(BF16)
        lo = (xs - hi.astype(F32)).astype(BF16)
        outs.append(jnp.dot(hi, ones_bd, preferred_element_type=F32)
                    + jnp.dot(lo, ones_bd, preferred_element_type=F32))
    return jnp.concatenate(outs, axis=1)


def _adaln_kernel(cv_ref, w_ref, b_ref, o_ref):
    cv = cv_ref[...]
    s = cv * _sigmoid(cv)
    o_ref[0] = _bdot(s, w_ref[0]) + b_ref[0]


def _adaln(cv, w_mod, b_mod):
    depth, d, n = w_mod.shape
    rows = cv.shape[0]
    tn = 1024
    return pl.pallas_call(
        _adaln_kernel,
        grid=(depth, n // tn),
        in_specs=[pl.BlockSpec((rows, d), lambda i, j: (0, 0)),
                  pl.BlockSpec((1, d, tn), lambda i, j: (i, 0, j)),
                  pl.BlockSpec((1, 1, tn), lambda i, j: (i, 0, j))],
        out_specs=pl.BlockSpec((1, rows, tn), lambda i, j: (i, 0, j)),
        out_shape=jax.ShapeDtypeStruct((depth, rows, n), F32),
        compiler_params=_cparams(("parallel", "parallel")),
        name="adaln",
    )(cv, w_mod, b_mod.reshape(depth, 1, n))


def _residual_in(x_ref, moe_refs):
    x = x_ref[0]
    if moe_refs is not None:
        y1_ref, y2_ref, gt_ref, mprev_ref = moe_refs
        gt = gt_ref[0]
        moe = gt[:, 0:1] * y1_ref[0].astype(F32) + gt[:, 1:2] * y2_ref[0].astype(F32)
        x = x + mprev_ref[0, 0][5:6] * moe
    return x


def _in_even_kernel(has_delta, nct, *refs):
    if has_delta:
        x = _residual_in(refs[0], refs[1:5])
        refs = refs[5:]
    else:
        x = jnp.where(pl.program_id(1) < nct, refs[0][0], refs[1][0])
        refs = refs[2:]
    mod_ref, g_ref, w_ref, xo_ref, z_ref = refs
    xo_ref[0] = x
    m = mod_ref[0, 0]
    h = _rms_mod(x, g_ref[...], m[0:1], m[1:2])
    z_ref[0] = jnp.dot(h.astype(BF16), w_ref[...], preferred_element_type=F32)


def _in_odd_kernel(n_ctx, *refs):
    x = _residual_in(refs[0], refs[1:5])
    mod_ref, g_ref, w_ref, cw_ref, xo_ref, v_ref = refs[5:]
    xo_ref[0] = x
    m = mod_ref[0, 0]
    h = _rms_mod(x, g_ref[...], m[0:1], m[1:2])
    z = jnp.dot(h.astype(BF16), w_ref[...], preferred_element_type=F32)
    tm, d = x.shape
    bg = z[:, 0:d]
    u = z[:, d:2 * d] * z[:, 2 * d:3 * d]
    pos = pl.program_id(1) * tm + lax.broadcasted_iota(jnp.int32, (tm, 1), 0)
    is_ctx = pos < n_ctx
    col = lax.rem(pos - n_ctx, GRID_W)
    prev_ok = jnp.where(is_ctx, pos, col) != 0
    next_ok = jnp.where(is_ctx, pos - (n_ctx - 1), col - (GRID_W - 1)) != 0
    u_prev = jnp.where(prev_ok, pltpu.roll(u, 1, axis=0), 0.0)
    u_next = jnp.where(next_ok, pltpu.roll(u, tm - 1, axis=0), 0.0)
    cw = cw_ref[...]
    conv = cw[0:1] * u_prev + cw[1:2] * u + cw[2:3] * u_next
    v_ref[0] = (bg * conv).astype(BF16)


def _tile_rows(n_ctx, seq):
    tm = 256
    while n_ctx % tm or seq % tm:
        tm //= 2
    assert tm % GRID_W == 0 and tm % CHUNK == 0
    return tm


def _in_proj(kind, x, delta, mod_prev, mod, gain, w, conv_w, n_ctx):
    has_delta = delta is not None
    if has_delta:
        b, t, d = x.shape
    else:
        b, t, d = x[0].shape[0], x[0].shape[1] + x[1].shape[1], x[0].shape[2]
    tm = _tile_rows(n_ctx, t - n_ctx)
    nct = n_ctx // tm
    n = w.shape[1]
    tok = pl.BlockSpec((1, tm, d), lambda i, j: (i, j, 0))
    modspec = pl.BlockSpec((1, 1, 6, d), lambda i, j: (i, jnp.where(j < nct, 0, 1), 0, 0))
    const2 = lambda shape: pl.BlockSpec(shape, lambda i, j: (0, 0), pipeline_mode=pl.Buffered(1))
    if has_delta:
        in_specs = [tok, tok, tok, pl.BlockSpec((1, tm, 2), lambda i, j: (i, j, 0)), modspec]
        args = [x, *delta, mod_prev]
    else:
        in_specs = [pl.BlockSpec((1, tm, d), lambda i, j: (i, jnp.minimum(j, nct - 1), 0)),
                    pl.BlockSpec((1, tm, d), lambda i, j: (i, jnp.maximum(j - nct, 0), 0))]
        args = [*x]
    in_specs += [modspec, const2((1, d)), const2((d, n))]
    args += [mod, gain.reshape(1, d), w]
    if kind == "even":
        body = functools.partial(_in_even_kernel, has_delta, nct)
        out_specs = [tok, pl.BlockSpec((1, tm, n), lambda i, j: (i, j, 0))]
        out_shape = [jax.ShapeDtypeStruct((b, t, d), F32), jax.ShapeDtypeStruct((b, t, n), F32)]
    else:
        assert has_delta
        body = functools.partial(_in_odd_kernel, n_ctx)
        in_specs.append(const2((3, d)))
        args.append(conv_w)
        out_specs = [tok, tok]
        out_shape = [jax.ShapeDtypeStruct((b, t, d), F32), jax.ShapeDtypeStruct((b, t, d), BF16)]
    return pl.pallas_call(
        body, grid=(b, t // tm), in_specs=in_specs, out_specs=out_specs, out_shape=out_shape,
        compiler_params=_cparams(("parallel", "parallel")), name="in_proj_" + kind,
    )(*args)


def _prepare_kernel(n_ctx, aw, zc_ref, zp_ref, zn_ref, mup_ref, mun_ref, w0_ref, w2_ref, a0_ref, a2_ref,
                    g2_ref, kk_ref, ka_ref, ones_ref,
                    r_ref, v_ref, kkn_ref, g_ref, ld_ref, kd_ref, icl_ref):
    z = zc_ref[0]
    tm = z.shape[0]
    j = pl.program_id(1)
    nct = n_ctx // tm
    first = jnp.logical_or(j == 0, j == nct)
    last = jnp.logical_or(j == nct - 1, j == pl.num_programs(1) - 1)
    row = lax.broadcasted_iota(jnp.int32, (tm, 1), 0)
    halo_p = jnp.where(first, 0.0, zp_ref[0][7:8])
    halo_n = jnp.where(last, 0.0, zn_ref[0][0:1])
    z_prev = jnp.where(row == 0, halo_p, pltpu.roll(z, 1, axis=0))
    z_next = jnp.where(row == tm - 1, halo_n, pltpu.roll(z, tm - 1, axis=0))
    z = z + mup_ref[...] * (z_prev - z) + mun_ref[...] * (z_next - z)
    r = z[:, 0:aw]
    k = z[:, aw:2 * aw]
    v = z[:, 2 * aw:3 * aw]
    o = 3 * aw
    dw = jnp.tanh(z[:, o:o + 2 * DECAY_LORA])
    da = z[:, o + 2 * DECAY_LORA:o + 2 * DECAY_LORA + 2 * ICL_LORA]
    dg = _sigmoid(z[:, o + 2 * DECAY_LORA + 2 * ICL_LORA:o + 2 * DECAY_LORA + 2 * ICL_LORA + GATE_LORA])
    r_ref[0] = r.astype(BF16)
    v_ref[0] = v.astype(BF16)
    g_ref[0] = _bdot(dg, g2_ref[...]).astype(BF16)
    kk = k * kk_ref[...]
    nrm = jnp.sqrt(_segsum(kk * kk, ones_ref[...]))
    kkn_ref[0] = (kk / jnp.maximum(nrm, 1e-12)).astype(BF16)
    for dr in range(2):
        x = -(w0_ref[dr:dr + 1] + _bdot(dw, w2_ref[dr]))
        softplus = jnp.maximum(x, 0.0) + jnp.log(1.0 + jnp.exp(-jnp.abs(x)))
        ld_ref[dr, 0] = -jnp.exp(-softplus - 0.5)
        icl = _sigmoid(a0_ref[dr:dr + 1] + _bdot(da, a2_ref[dr]))
        icl_ref[dr, 0] = icl.astype(BF16)
        kd_ref[dr, 0] = (k * (1.0 + (icl - 1.0) * ka_ref[...])).astype(BF16)


def _pad_lora(w):
    z = jnp.zeros_like(w[0])
    return jnp.stack([jnp.concatenate([w[0], z], 0), jnp.concatenate([z, w[1]], 0)]).astype(BF16)


def _ones_blockdiag():
    i = jnp.arange(LANE) // A_HEAD
    return (i[:, None] == i[None, :]).astype(BF16)


def _prepare(z, mu_prev, mu_next, w0, w2, a0, a2, g2, k_k, k_a, n_ctx):
    b, t, _ = z.shape
    aw = w0.shape[1]
    ap = mu_prev.shape[0]
    tm = _tile_rows(n_ctx, t - n_ctx)
    hb = tm // 8
    nhb = t // 8
    tok = lambda i, j: (i, j, 0)
    c2 = lambda shape: pl.BlockSpec(shape, lambda i, j: (0,) * len(shape))
    in_specs = [pl.BlockSpec((1, tm, ap), tok),
                pl.BlockSpec((1, 8, ap), lambda i, j: (i, jnp.maximum(j * hb - 1, 0), 0)),
                pl.BlockSpec((1, 8, ap), lambda i, j: (i, jnp.minimum((j + 1) * hb, nhb - 1), 0)),
                c2((1, ap)), c2((1, ap)), c2((2, aw)), c2((2, 2 * DECAY_LORA, aw)), c2((2, aw)),
                c2((2, 2 * ICL_LORA, aw)), c2((GATE_LORA, aw)), c2((1, aw)), c2((1, aw)), c2((LANE, LANE))]
    one = pl.BlockSpec((1, tm, aw), tok)
    two = pl.BlockSpec((2, 1, tm, aw), lambda i, j: (0, i, j, 0))
    s1 = jax.ShapeDtypeStruct((b, t, aw), BF16)
    s2 = jax.ShapeDtypeStruct((2, b, t, aw), BF16)
    return pl.pallas_call(
        functools.partial(_prepare_kernel, n_ctx, aw),
        grid=(b, t // tm), in_specs=in_specs,
        out_specs=[one, one, one, one, two, two, two],
        out_shape=[s1, s1, s1, s1, jax.ShapeDtypeStruct((2, b, t, aw), F32), s2, s2],
        compiler_params=_cparams(("parallel", "parallel")), name="rwkv_prepare",
    )(z, z, z, mu_prev.reshape(1, ap), mu_next.reshape(1, ap), w0, _pad_lora(w2), a0, _pad_lora(a2),
      g2.astype(BF16), k_k.reshape(1, aw), k_a.reshape(1, aw), _ones_blockdiag())


def _scan_kernel(*refs):
    ins, (yf_ref, yb_ref, z_scr) = refs[:12], refs[12:]
    step = pl.program_id(1)
    c = CHUNK

    @pl.when(step == 0)
    def _():
        z_scr[...] = jnp.zeros_like(z_scr)

    ti = lax.broadcasted_iota(jnp.int32, (c, c), 0)
    tj = lax.broadcasted_iota(jnp.int32, (c, c), 1)
    l64 = lax.broadcasted_iota(jnp.int32, (c, PAIR), 1)
    r64 = lax.broadcasted_iota(jnp.int32, (c, PAIR), 0)
    lane_a = l64 < A_HEAD
    li = jnp.where(lane_a, l64, l64 - A_HEAD)
    eye_pl = jnp.where(li == r64, 1.0, 0.0)

    a_t, r_t, b_t, k_t, b_h, k_h, p_c, vv, before, before_eq = ([] for _ in range(10))
    for d in range(2):
        r_ref, v_ref, kk_ref, ld_ref, kd_ref, icl_ref = ins[6 * d:6 * d + 6]
        fwd = d == 0
        tri = jnp.where((tj <= ti) if fwd else (tj >= ti), 1.0, 0.0).astype(BF16)
        ld = ld_ref[0, 0]
        ld_hi = ld.astype(BF16)
        ld_lo = (ld - ld_hi.astype(F32)).astype(BF16)
        linc = (jnp.dot(tri, ld_hi, preferred_element_type=F32)
                + jnp.dot(tri, ld_lo, preferred_element_type=F32))
        lexc = linc - ld
        ltot = linc[c - 1:c] if fwd else linc[0:1]
        kk = kk_ref[0].astype(F32)
        bvec = kk * icl_ref[0, 0].astype(F32)
        kd = kd_ref[0, 0].astype(F32)
        e_neg = jnp.exp(-linc)
        e_tot = jnp.exp(ltot - linc)
        a_t.append(-kk * jnp.exp(lexc))
        r_t.append(r_ref[0].astype(F32) * jnp.exp(linc))
        b_t.append(bvec * e_neg)
        k_t.append(kd * e_neg)
        b_h.append(bvec * e_tot)
        k_h.append(kd * e_tot)
        p_c.append(jnp.exp(ltot))
        vv.append(v_ref[0].astype(F32))
        bf = (li < r64) if fwd else (li > r64)
        before.append(bf)
        before_eq.append(jnp.logical_or(bf, li == r64))

    r128 = lax.broadcasted_iota(jnp.int32, (PAIR, PAIR), 0)
    l128 = lax.broadcasted_iota(jnp.int32, (PAIR, PAIR), 1)
    diag_blk = (r128 < A_HEAD) == (l128 < A_HEAD)
    eye128 = r128 == l128
    lane_a2 = l128 < A_HEAD

    def bd(x):
        x = x.astype(F32)
        return jnp.where(diag_blk, jnp.concatenate([x, x], axis=0), 0.0).astype(BF16)

    def abd(x):
        x = x.astype(F32)
        return jnp.where(diag_blk, 0.0, jnp.concatenate([x, x], axis=0)).astype(BF16)

    nt = (((1,), (1,)), ((), ()))
    n_pairs = yf_ref.shape[2] // PAIR
    dirs = [d for d in range(2) for _ in range(n_pairs)]
    sls = [slice(p * PAIR, (p + 1) * PAIR) for _ in range(2) for p in range(n_pairs)]
    chains = list(zip(dirs, sls))
    mm = lambda a, b: jnp.dot(a.astype(BF16), b, preferred_element_type=F32)
    g1, g2 = [], []
    for d, sl in chains:
        ar = jnp.concatenate([a_t[d][:, sl], r_t[d][:, sl]], axis=0)
        bk = jnp.concatenate([b_t[d][:, sl], k_t[d][:, sl]], axis=0).astype(BF16)
        kb = jnp.concatenate([k_t[d][:, sl], b_t[d][:, sl]], axis=0).astype(BF16)
        g1.append(lax.dot_general(jnp.where(lane_a2, ar, 0.0).astype(BF16), bk, nt, preferred_element_type=F32))
        g2.append(lax.dot_general(jnp.where(lane_a2, 0.0, ar).astype(BF16), kb, nt, preferred_element_type=F32))
    sel_a = [jnp.logical_and(m, lane_a) for m in before]
    sel_a_eq = [jnp.logical_and(m, lane_a) for m in before_eq]
    ab = [jnp.where(sel_a[d], x[:c], jnp.where(before[d], y[:c], 0.0)) for d, x, y in zip(dirs, g1, g2)]
    rb = [jnp.where(sel_a_eq[d], x[c:], jnp.where(before_eq[d], y[c:], 0.0)) for d, x, y in zip(dirs, g1, g2)]
    ak_s = [jnp.where(sel_a[d], y[:c], jnp.where(before[d], x[:c], 0.0)) for d, x, y in zip(dirs, g1, g2)]
    rk_s = [jnp.where(sel_a_eq[d], y[c:], jnp.where(before_eq[d], x[c:], 0.0)) for d, x, y in zip(dirs, g1, g2)]
    akv_rkv = [mm(jnp.concatenate([a, b], axis=0), abd(vv[d][:, sl])) for a, b, (d, sl) in zip(ak_s, rk_s, chains)]
    s = [eye_pl + x for x in ab]
    q = [mm(x, bd(x)) for x in ab]
    n = 2
    while n < c:
        if 2 * n >= c:
            s = [si + mm(qi, bd(si)) for qi, si in zip(q, s)]
        else:
            out = [mm(qi, jnp.concatenate([bd(qi), bd(si)], axis=1)) for qi, si in zip(q, s)]
            s = [si + o[:, PAIR:] for si, o in zip(s, out)]
            q = [o[:, :PAIR] for o in out]
        n *= 2
    ah_w2 = [mm(si, jnp.concatenate([bd(a_t[d][:, sl]), bd(x[:c])], axis=1))
             for si, (d, sl), x in zip(s, chains, akv_rkv)]
    rb_o = [mm(x, jnp.concatenate([bd(o[:, :PAIR]), bd(o[:, PAIR:])], axis=1)) for x, o in zip(rb, ah_w2)]
    pg = []
    for (d, sl), o in zip(chains, ah_w2):
        v_p = vv[d][:, sl]
        bkh_t = jnp.concatenate([b_h[d][:, sl], k_h[d][:, sl]], axis=0).T
        rhs = jnp.concatenate([o, jnp.concatenate([jnp.zeros_like(v_p), v_p], axis=1)], axis=0)
        pg.append(mm(bkh_t, rhs.astype(BF16)))
    ys, zs = [], []
    for i, (d, sl) in enumerate(chains):
        rbar = r_t[d][:, sl] + rb_o[i][:, :PAIR]
        y0 = rb_o[i][:, PAIR:] + akv_rkv[i][c:]
        pc_row = jnp.broadcast_to(p_c[d][:, sl], (PAIR, PAIR))
        phi = jnp.where(diag_blk, pg[i][:, :PAIR], 0.0) + jnp.where(eye128, pc_row, 0.0)
        gam = jnp.where(diag_blk, pg[i][:, PAIR:], 0.0)
        yz = mm(jnp.concatenate([rbar, phi], axis=0), z_scr[i].astype(BF16))
        ys.append(yz[:c] + y0)
        zs.append(yz[c:] + gam)
    yf_ref[0] = jnp.concatenate(ys[:n_pairs], axis=1)
    yb_ref[0] = jnp.concatenate(ys[n_pairs:], axis=1)
    for i in range(len(chains)):
        z_scr[i] = zs[i]


def _scan(r, v, kkn, ld, kd, icl, n_ctx):
    b, t, aw = r.shape
    nch = t // CHUNK
    ncc = n_ctx // CHUNK

    def chunk_of(d, i):
        return i if d == 0 else jnp.where(i < ncc, ncc - 1 - i, nch - 1 - (i - ncc))

    in_specs, args = [], []
    for d in range(2):
        one = pl.BlockSpec((1, CHUNK, aw), lambda bi, i, d=d: (bi, chunk_of(d, i), 0))
        two = pl.BlockSpec((1, 1, CHUNK, aw), lambda bi, i, d=d: (d, bi, chunk_of(d, i), 0))
        in_specs += [one, one, one, two, two, two]
        args += [r, v, kkn, ld, kd, icl]
    out_specs = [pl.BlockSpec((1, CHUNK, aw), lambda bi, i, d=d: (bi, chunk_of(d, i), 0)) for d in range(2)]
    return pl.pallas_call(
        _scan_kernel,
        grid=(b, nch),
        in_specs=in_specs,
        out_specs=out_specs,
        out_shape=[jax.ShapeDtypeStruct((b, t, aw), F32)] * 2,
        scratch_shapes=[pltpu.VMEM((2 * (aw // PAIR), PAIR, PAIR), F32)],
        compiler_params=_cparams(("parallel", "arbitrary")), name="wkv7_scan",
    )(*args)


def _conv_kernel(n_ctx, width, u_ref, gate_ref, w_ref, bias_ref, o_ref, lat_scr, ctx_scr):
    half = (width - 1) // 2
    t = u_ref.shape[1]
    seq = t - n_ctx
    padl = half * GRID_W
    padc = 16
    lat_scr[0:padl] = jnp.zeros((padl, LANE), F32)
    lat_scr[padl + seq:] = jnp.zeros((padl, LANE), F32)
    lat_scr[padl:padl + seq] = u_ref[0, n_ctx:] * _sigmoid(gate_ref[0, n_ctx:])
    ctx_scr[0:padc] = jnp.zeros((padc, LANE), F32)
    ctx_scr[padc + n_ctx:] = jnp.zeros((padc, LANE), F32)
    ctx_scr[padc:padc + n_ctx] = u_ref[0, 0:n_ctx] * _sigmoid(gate_ref[0, 0:n_ctx])
    bias = bias_ref[...]
    acc = jnp.zeros((n_ctx, LANE), F32)
    for j in range(width):
        acc = acc + w_ref[j:j + 1] * ctx_scr[padc - half + j:padc - half + j + n_ctx]
    o_ref[0, 0:n_ctx] = acc + bias

    def body(rc, carry):
        acc = jnp.zeros((GRID_W, LANE), F32)
        for j in range(width):
            acc = acc + w_ref[j:j + 1] * lat_scr[pl.ds(pl.multiple_of((rc + j) * GRID_W, GRID_W), GRID_W)]
        o_ref[0, pl.ds(pl.multiple_of(n_ctx + rc * GRID_W, GRID_W), GRID_W)] = acc + bias
        return carry

    lax.fori_loop(0, seq // GRID_W, body, 0)


def _conformer_conv(z, col0, conv_w, conv_bias, n_ctx):
    b, t, _ = z.shape
    width, bw = conv_w.shape
    half = (width - 1) // 2
    nb = bw // LANE
    c0 = col0 // LANE
    return pl.pallas_call(
        functools.partial(_conv_kernel, n_ctx, width),
        grid=(b, nb),
        in_specs=[pl.BlockSpec((1, t, LANE), lambda i, j: (i, 0, c0 + j)),
                  pl.BlockSpec((1, t, LANE), lambda i, j: (i, 0, c0 + nb + j)),
                  pl.BlockSpec((width, LANE), lambda i, j: (0, j)),
                  pl.BlockSpec((1, LANE), lambda i, j: (0, j))],
        out_specs=pl.BlockSpec((1, t, LANE), lambda i, j: (i, 0, j)),
        out_shape=jax.ShapeDtypeStruct((b, t, bw), F32),
        scratch_shapes=[pltpu.VMEM((t - n_ctx + 2 * half * GRID_W, LANE), F32),
                        pltpu.VMEM((n_ctx + 32, LANE), F32)],
        compiler_params=_cparams(("parallel", "parallel")), name="conformer_conv",
    )(z, z, conv_w, conv_bias.reshape(1, bw))


def _merge_kernel(yf_ref, yb_ref, r_ref, v_ref, kd_ref, g_ref, ub_ref, rk_ref, lg_ref, lb_ref, bg_ref, bb_ref,
                  ones_ref, o_ref):
    aw = r_ref.shape[2]
    ones = ones_ref[...]
    y = yf_ref[0] + yb_ref[0]
    inv = 1.0 / A_HEAD
    mean = _segsum(y, ones) * inv
    yc = y - mean
    var = _segsum(yc * yc, ones) * inv
    yn = yc * lax.rsqrt(var + GN_EPS) * lg_ref[...] + lb_ref[...]
    kd_sum = kd_ref[0, 0].astype(F32) + kd_ref[1, 0].astype(F32)
    bonus = _segsum(r_ref[0].astype(F32) * kd_sum * rk_ref[...], ones) * v_ref[0].astype(F32)
    o_ref[0, :, 0:aw] = ((yn + bonus) * g_ref[0].astype(F32)).astype(BF16)
    u = ub_ref[0]
    mu = jnp.mean(u, axis=-1, keepdims=True)
    uc = u - mu
    uv = jnp.mean(uc * uc, axis=-1, keepdims=True)
    un = uc * lax.rsqrt(uv + LN_EPS) * bg_ref[...] + bb_ref[...]
    o_ref[0, :, aw:] = (un * _sigmoid(un)).astype(BF16)


def _merge(y, r, v, kd, g, ub, r_k, lnx_g, lnx_b, lnb_g, lnb_b, n_ctx):
    b, t, aw = r.shape
    bw = ub.shape[2]
    tm = _tile_rows(n_ctx, t - n_ctx)
    one = pl.BlockSpec((1, tm, aw), lambda i, j: (i, j, 0))
    two = pl.BlockSpec((2, 1, tm, aw), lambda i, j: (0, i, j, 0))
    c2 = lambda shape: pl.BlockSpec(shape, lambda i, j: (0, 0))
    return pl.pallas_call(
        _merge_kernel, grid=(b, t // tm),
        in_specs=[one, one, one, one, two, one, pl.BlockSpec((1, tm, bw), lambda i, j: (i, j, 0)),
                  c2((1, aw)), c2((1, aw)), c2((1, aw)), c2((1, bw)), c2((1, bw)), c2((LANE, LANE))],
        out_specs=pl.BlockSpec((1, tm, aw + bw), lambda i, j: (i, j, 0)),
        out_shape=jax.ShapeDtypeStruct((b, t, aw + bw), BF16),
        compiler_params=_cparams(("parallel", "parallel")), name="rwkv_readout_conformer_norm",
    )(*y, r, v, kd, g, ub, r_k.reshape(1, aw), lnx_g.reshape(1, aw), lnx_b.reshape(1, aw),
      lnb_g.reshape(1, bw), lnb_b.reshape(1, bw), _ones_blockdiag())


def _out_kernel(lhs_ref, x_ref, mod_ref, g_ref, w_ref, wr_ref, rb_ref, xo_ref, h_ref, eid_ref, gate_ref):
    m = mod_ref[0, 0]
    x = x_ref[0] + m[2:3] * jnp.dot(lhs_ref[0], w_ref[...], preferred_element_type=F32)
    xo_ref[0] = x
    h = _rms_mod(x, g_ref[...], m[3:4], m[4:5]).astype(BF16)
    h_ref[0] = h
    logits = lax.dot_general(wr_ref[...], h, (((1,), (1,)), ((), ())), preferred_element_type=F32)
    scores = _sigmoid(logits)
    sel = scores + rb_ref[...]
    e = EXPERTS_PER_GROUP
    best = None
    for gi in range(N_GROUPS):
        rows = [sel[gi * e + i:gi * e + i + 1] for i in range(e)]
        top2 = None
        for i in range(e):
            for k in range(i + 1, e):
                s = rows[i] + rows[k]
                top2 = s if top2 is None else jnp.maximum(top2, s)
        if best is None:
            best, best_v = jnp.zeros_like(top2, dtype=jnp.int32), top2
        else:
            upd = top2 > best_v
            best = jnp.where(upd, gi, best)
            best_v = jnp.where(upd, top2, best_v)

    def pick(a, i):
        out = a[i:i + 1]
        for gi in range(1, N_GROUPS):
            out = jnp.where(best == gi, a[gi * e + i:gi * e + i + 1], out)
        return out

    cs = [pick(sel, i) for i in range(e)]
    sc = [pick(scores, i) for i in range(e)]

    def argmax4(vals):
        idx, val = jnp.zeros_like(best), vals[0]
        for i in range(1, e):
            upd = vals[i] > val
            idx = jnp.where(upd, i, idx)
            val = jnp.where(upd, vals[i], val)
        return idx

    i1 = argmax4(cs)
    i2 = argmax4([jnp.where(i1 == i, -jnp.inf, cs[i]) for i in range(e)])

    def take(vals, idx):
        out = vals[0]
        for i in range(1, e):
            out = jnp.where(idx == i, vals[i], out)
        return out

    s1, s2 = take(sc, i1), take(sc, i2)
    tot = s1 + s2
    eid_ref[0] = jnp.concatenate([best * e + i1, best * e + i2], axis=0)
    gate_ref[0] = jnp.concatenate([s1 / tot, s2 / tot], axis=0)


def _out_proj(lhs, x, mod, gain, w, w_router, router_bias, n_ctx):
    b, t, d = x.shape
    tm = _tile_rows(n_ctx, t - n_ctx)
    nct = n_ctx // tm
    k = lhs.shape[2]
    ne = w_router.shape[1]
    tok = lambda i, j: (i, j, 0)
    c2 = lambda shape: pl.BlockSpec(shape, lambda i, j: (0, 0), pipeline_mode=pl.Buffered(1))
    return pl.pallas_call(
        _out_kernel, grid=(b, t // tm),
        in_specs=[pl.BlockSpec((1, tm, k), tok), pl.BlockSpec((1, tm, d), tok),
                  pl.BlockSpec((1, 1, 6, d), lambda i, j: (i, jnp.where(j < nct, 0, 1), 0, 0)),
                  c2((1, d)), c2((k, d)), c2((ne, d)), c2((ne, 1))],
        out_specs=[pl.BlockSpec((1, tm, d), tok), pl.BlockSpec((1, tm, d), tok),
                   pl.BlockSpec((1, 2, tm), lambda i, j: (i, 0, j)),
                   pl.BlockSpec((1, 2, tm), lambda i, j: (i, 0, j))],
        out_shape=[jax.ShapeDtypeStruct((b, t, d), F32), jax.ShapeDtypeStruct((b, t, d), BF16),
                   jax.ShapeDtypeStruct((b, 2, t), jnp.int32), jax.ShapeDtypeStruct((b, 2, t), F32)],
        compiler_params=_cparams(("parallel", "parallel")), name="out_proj_router",
    )(lhs, x, mod, gain.reshape(1, d), w, w_router.T.astype(BF16), router_bias.reshape(ne, 1))


def _expert_kernel(it_ref, ie_ref, fl_ref, lo_ref, hi_ref, ni_ref, x_ref, wg_ref, wu_ref, wd_ref, o_ref,
                   wg_s, wu_s, wd_s):
    j = pl.program_id(0)
    tm = x_ref.shape[0]

    @pl.when(j < ni_ref[0])
    def _():
        @pl.when((fl_ref[j] & 1) != 0)
        def _():
            wg_s[...] = wg_ref[0, 0].astype(BF16)
            wu_s[...] = wu_ref[0, 0].astype(BF16)
            wd_s[...] = wd_ref[0, 0].astype(BF16)

        x = x_ref[...]
        a = jnp.dot(x, wg_s[...], preferred_element_type=F32)
        u = jnp.dot(x, wu_s[...], preferred_element_type=F32)
        he = (a * _sigmoid(a) * u).astype(BF16)
        y = jnp.dot(he, wd_s[...], preferred_element_type=F32).astype(BF16)
        row = it_ref[j] * tm + lax.broadcasted_iota(jnp.int32, (tm, 1), 0)
        mine = jnp.logical_and(row >= lo_ref[j], row < hi_ref[j])

        @pl.when((fl_ref[j] & 2) != 0)
        def _():
            o_ref[...] = jnp.where(mine, y, jnp.zeros_like(y))

        @pl.when((fl_ref[j] & 2) == 0)
        def _():
            o_ref[...] = jnp.where(mine, y, o_ref[...])


def _experts(x_sorted, items, layer, w_gate, w_up, w_down, tm):
    p, d = x_sorted.shape
    f = w_gate.shape[3]
    n_items = items[0].shape[0]
    wspec = lambda shape: pl.BlockSpec((1, 1) + shape, lambda j, it, ie, *_: (layer, ie[j], 0, 0),
                                       pipeline_mode=pl.Buffered(1))
    grid_spec = pltpu.PrefetchScalarGridSpec(
        num_scalar_prefetch=6, grid=(n_items,),
        in_specs=[pl.BlockSpec((tm, d), lambda j, it, *_: (it[j], 0)), wspec((d, f)), wspec((d, f)), wspec((f, d))],
        out_specs=pl.BlockSpec((tm, d), lambda j, it, *_: (it[j], 0)),
        scratch_shapes=[pltpu.VMEM((d, f), BF16), pltpu.VMEM((d, f), BF16), pltpu.VMEM((f, d), BF16)])
    return pl.pallas_call(
        _expert_kernel, grid_spec=grid_spec,
        out_shape=jax.ShapeDtypeStruct((p, d), BF16),
        compiler_params=_cparams(("arbitrary",)), name="moe_experts",
    )(*items, x_sorted, w_gate, w_up, w_down)


def _moe(h2, eid, gates, layer, w_gate, w_up, w_down, tm):
    b, t, d = h2.shape
    n = b * t
    ne = w_gate.shape[1]
    e_flat = eid.transpose(0, 2, 1).reshape(n * 2)
    slot = jnp.arange(2 * n, dtype=jnp.int32)
    sorted_e, order = lax.sort((e_flat, slot), num_keys=1, is_stable=True)
    _, pos = lax.sort((order, slot), num_keys=1)
    bounds = jnp.sum(sorted_e[None, :] < jnp.arange(ne + 1, dtype=jnp.int32)[:, None], axis=1).astype(jnp.int32)
    lo_t = bounds[:-1] // tm
    n_e = jnp.where(bounds[1:] > bounds[:-1], (bounds[1:] - 1) // tm - lo_t + 1, 0)
    item_end = jnp.cumsum(n_e)
    item_start = item_end - n_e
    n_items = item_end[-1]
    max_items = (2 * n) // tm + ne - 1
    jj = jnp.minimum(jnp.arange(max_items, dtype=jnp.int32), n_items - 1)
    ie = jnp.minimum(jnp.sum(item_end[None, :] <= jj[:, None], axis=1), ne - 1).astype(jnp.int32)
    onehot = ie[:, None] == jnp.arange(ne, dtype=jnp.int32)[None, :]
    pick = lambda tab: jnp.sum(jnp.where(onehot, tab[None, :], 0), axis=1).astype(jnp.int32)
    it = pick(lo_t) + jj - pick(item_start)
    first = jnp.arange(max_items) == 0
    new_e = jnp.logical_or(first, ie != jnp.roll(ie, 1))
    new_t = jnp.logical_or(first, it != jnp.roll(it, 1))
    flags = new_e.astype(jnp.int32) + 2 * new_t.astype(jnp.int32)
    items = (it, ie, flags, pick(bounds[:-1]), pick(bounds[1:]), n_items.reshape(1).astype(jnp.int32))
    x_sorted = jnp.take(h2.reshape(n, d), order // 2, axis=0, mode="clip")
    y_sorted = _experts(x_sorted, items, layer, w_gate, w_up, w_down, tm)
    pos2 = pos.reshape(n, 2)
    y1 = jnp.take(y_sorted, pos2[:, 0], axis=0, mode="clip").reshape(b, t, d)
    y2 = jnp.take(y_sorted, pos2[:, 1], axis=0, mode="clip").reshape(b, t, d)
    return y1, y2, gates.transpose(0, 2, 1)


def _final_kernel(x_ref, y1_ref, y2_ref, gt_ref, mprev_ref, g_ref, o_ref):
    x = _residual_in(x_ref, (y1_ref, y2_ref, gt_ref, mprev_ref))
    o_ref[0] = x * lax.rsqrt(jnp.mean(x * x, axis=-1, keepdims=True) + RMS_EPS) * g_ref[...]


def _final(x, delta, mod_prev, gain, n_ctx):
    b, t, d = x.shape
    tm = _tile_rows(n_ctx, t - n_ctx)
    nct = n_ctx // tm
    tok = pl.BlockSpec((1, tm, d), lambda i, j: (i, j + nct, 0))
    return pl.pallas_call(
        _final_kernel, grid=(b, (t - n_ctx) // tm),
        in_specs=[tok, tok, tok, pl.BlockSpec((1, tm, 2), lambda i, j: (i, j + nct, 0)),
                  pl.BlockSpec((1, 1, 6, d), lambda i, j: (i, 1, 0, 0)),
                  pl.BlockSpec((1, d), lambda i, j: (0, 0))],
        out_specs=pl.BlockSpec((1, tm, d), lambda i, j: (i, j, 0)),
        out_shape=jax.ShapeDtypeStruct((b, t - n_ctx, d), F32),
        compiler_params=_cparams(("parallel", "parallel")), name="final_norm",
    )(x, *delta, mod_prev, gain.reshape(1, d))


def kernel(x, c, ctx, c_ctx, norm1_g, norm2_g, w_mod, b_mod, w_in_e, mu_prev, mu_next, w0, w2, a0, a2, g2,
           k_k, k_a, r_k, lnx_g, lnx_b, conv_b_w, conv_b_bias, lnb_g, lnb_b, w_out_e, w_in_o, conv_c_w,
           w_out_o, w_router, router_bias, w_gate, w_up, w_down, final_g):
    b, s, d = x.shape
    n_ctx = ctx.shape[1]
    depth = w_mod.shape[0]
    tm = _tile_rows(n_ctx, s)
    aw = w0.shape[2]
    a_proj = mu_prev.shape[1]

    xa = (ctx, x)
    rows = ((b + 1 + 7) // 8) * 8
    cv = jnp.zeros((rows, d), F32).at[:b].set(c).at[b].set(c_ctx)
    mods_all = _adaln(cv, w_mod, b_mod).reshape(depth, rows, 6, d)
    mods = [jnp.stack([jnp.broadcast_to(mods_all[i, b], (b, 6, d)), mods_all[i, :b]], axis=1)
            for i in range(depth)]

    delta = None
    for i in range(depth):
        j = i // 2
        mod_prev = mods[i - 1] if i else None
        if i % 2 == 0:
            xa, z = _in_proj("even", xa, delta, mod_prev, mods[i], norm1_g[i], w_in_e[j].astype(BF16), None,
                             n_ctx)
            r, v, kkn, g, ld, kd, icl = _prepare(z, mu_prev[j], mu_next[j], w0[j], w2[j], a0[j], a2[j], g2[j],
                                                 k_k[j], k_a[j], n_ctx)
            y = _scan(r, v, kkn, ld, kd, icl, n_ctx)
            ub = _conformer_conv(z, a_proj, conv_b_w[j], conv_b_bias[j], n_ctx)
            lhs = _merge(y, r, v, kd, g, ub, r_k[j].reshape(aw), lnx_g[j], lnx_b[j], lnb_g[j], lnb_b[j], n_ctx)
            w_out = w_out_e[j]
        else:
            xa, lhs = _in_proj("odd", xa, delta, mod_prev, mods[i], norm1_g[i], w_in_o[j].astype(BF16),
                               conv_c_w[j], n_ctx)
            w_out = w_out_o[j]
        xa, h2, eid, gates = _out_proj(lhs, xa, mods[i], norm2_g[i], w_out.astype(BF16), w_router, router_bias,
                                       n_ctx)
        delta = _moe(h2, eid, gates, i, w_gate, w_up, w_down, tm)
    return _final(xa, delta, mods[depth - 1], final_g, n_ctx)
```

```python
import functools

import jax
import jax.numpy as jnp
from jax import lax
from jax.experimental import pallas as pl
from jax.experimental.pallas import tpu as pltpu

F32 = jnp.float32
BF16 = jnp.bfloat16

GRID_W = 64
A_HEAD = 64
DECAY_LORA = 64
ICL_LORA = 64
GATE_LORA = 128
N_EXPERTS = 16
N_GROUPS = 4
EXPERTS_PER_GROUP = N_EXPERTS // N_GROUPS
RMS_EPS = 1e-6
LN_EPS = 1e-5
GN_EPS = 64e-5

LANE = 128
CHUNK = 64
PAIR = 2 * A_HEAD
VMEM_LIMIT = 56 * 1024 * 1024


def _cparams(sem):
    return pltpu.CompilerParams(dimension_semantics=sem, vmem_limit_bytes=VMEM_LIMIT)


def _sigmoid(x):
    return 1.0 / (1.0 + jnp.exp(-x))


def _bdot(a, b):
    return jnp.dot(a.astype(BF16), b.astype(BF16), preferred_element_type=F32)


def _rms_mod(x, g, shift, scale):
    y = x * lax.rsqrt(jnp.mean(x * x, axis=-1, keepdims=True) + RMS_EPS)
    return (y * g) * (1.0 + scale) + shift


def _segsum(x, ones_bd):
    outs = []
    for c in range(x.shape[1] // LANE):
        xs = x[:, c * LANE:(c + 1) * LANE]
        hi = xs.astype(BF16)
        lo = (xs - hi.astype(F32)).astype(BF16)
        outs.append(jnp.dot(hi, ones_bd, preferred_element_type=F32)
                    + jnp.dot(lo, ones_bd, preferred_element_type=F32))
    return jnp.concatenate(outs, axis=1)


def _adaln_kernel(cv_ref, w_ref, b_ref, o_ref):
    cv = cv_ref[...]
    s = cv * _sigmoid(cv)
    o_ref[0] = _bdot(s, w_ref[0]) + b_ref[0]


def _adaln(cv, w_mod, b_mod):
    depth, d, n = w_mod.shape
    rows = cv.shape[0]
    tn = 1024
    return pl.pallas_call(
        _adaln_kernel,
        grid=(depth, n // tn),
        in_specs=[pl.BlockSpec((rows, d), lambda i, j: (0, 0)),
                  pl.BlockSpec((1, d, tn), lambda i, j: (i, 0, j)),
                  pl.BlockSpec((1, 1, tn), lambda i, j: (i, 0, j))],
        out_specs=pl.BlockSpec((1, rows, tn), lambda i, j: (i, 0, j)),
        out_shape=jax.ShapeDtypeStruct((depth, rows, n), F32),
        compiler_params=_cparams(("parallel", "parallel")),
        name="adaln",
    )(cv, w_mod, b_mod.reshape(depth, 1, n))


def _residual_in(x_ref, moe_refs):
    x = x_ref[0]
    if moe_refs is not None:
        y_ref, gt_ref, mprev_ref = moe_refs
        gt = gt_ref[0]
        d = x.shape[1]
        moe = gt[:, 0:1] * y_ref[0, :, 0:d].astype(F32) + gt[:, 1:2] * y_ref[0, :, d:].astype(F32)
        x = x + mprev_ref[0, 0][5:6] * moe
    return x


def _in_even_kernel(has_delta, nct, *refs):
    if has_delta:
        x = _residual_in(refs[0], refs[1:4])
        refs = refs[4:]
    else:
        x = jnp.where(pl.program_id(1) < nct, refs[0][0], refs[1][0])
        refs = refs[2:]
    mod_ref, g_ref, w_ref, xo_ref, z_ref = refs
    xo_ref[0] = x
    m = mod_ref[0, 0]
    h = _rms_mod(x, g_ref[...], m[0:1], m[1:2])
    z_ref[0] = jnp.dot(h.astype(BF16), w_ref[...], preferred_element_type=F32)


def _in_odd_kernel(n_ctx, *refs):
    x = _residual_in(refs[0], refs[1:4])
    mod_ref, g_ref, w_ref, cw_ref, xo_ref, v_ref = refs[4:]
    xo_ref[0] = x
    m = mod_ref[0, 0]
    h = _rms_mod(x, g_ref[...], m[0:1], m[1:2])
    z = jnp.dot(h.astype(BF16), w_ref[...], preferred_element_type=F32)
    tm, d = x.shape
    bg = z[:, 0:d]
    u = z[:, d:2 * d] * z[:, 2 * d:3 * d]
    pos = pl.program_id(1) * tm + lax.broadcasted_iota(jnp.int32, (tm, 1), 0)
    is_ctx = pos < n_ctx
    col = lax.rem(pos - n_ctx, GRID_W)
    prev_ok = jnp.where(is_ctx, pos, col) != 0
    next_ok = jnp.where(is_ctx, pos - (n_ctx - 1), col - (GRID_W - 1)) != 0
    u_prev = jnp.where(prev_ok, pltpu.roll(u, 1, axis=0), 0.0)
    u_next = jnp.where(next_ok, pltpu.roll(u, tm - 1, axis=0), 0.0)
    cw = cw_ref[...]
    conv = cw[0:1] * u_prev + cw[1:2] * u + cw[2:3] * u_next
    v_ref[0] = (bg * conv).astype(BF16)


def _tile_rows(n_ctx, seq):
    tm = 256
    while n_ctx % tm or seq % tm:
        tm //= 2
    assert tm % GRID_W == 0 and tm % CHUNK == 0
    return tm


def _in_proj(kind, x, delta, mod_prev, mod, gain, w, conv_w, n_ctx):
    has_delta = delta is not None
    if has_delta:
        b, t, d = x.shape
    else:
        b, t, d = x[0].shape[0], x[0].shape[1] + x[1].shape[1], x[0].shape[2]
    tm = _tile_rows(n_ctx, t - n_ctx)
    nct = n_ctx // tm
    n = w.shape[1]
    tok = pl.BlockSpec((1, tm, d), lambda i, j: (i, j, 0))
    modspec = pl.BlockSpec((1, 1, 6, d), lambda i, j: (i, jnp.where(j < nct, 0, 1), 0, 0))
    const2 = lambda shape: pl.BlockSpec(shape, lambda i, j: (0, 0), pipeline_mode=pl.Buffered(1))
    if has_delta:
        in_specs = [tok, pl.BlockSpec((1, tm, 2 * d), lambda i, j: (i, j, 0)),
                    pl.BlockSpec((1, tm, 2), lambda i, j: (i, j, 0)), modspec]
        args = [x, *delta, mod_prev]
    else:
        in_specs = [pl.BlockSpec((1, tm, d), lambda i, j: (i, jnp.minimum(j, nct - 1), 0)),
                    pl.BlockSpec((1, tm, d), lambda i, j: (i, jnp.maximum(j - nct, 0), 0))]
        args = [*x]
    in_specs += [modspec, const2((1, d)), const2((d, n))]
    args += [mod, gain.reshape(1, d), w]
    if kind == "even":
        body = functools.partial(_in_even_kernel, has_delta, nct)
        out_specs = [tok, pl.BlockSpec((1, tm, n), lambda i, j: (i, j, 0))]
        out_shape = [jax.ShapeDtypeStruct((b, t, d), F32), jax.ShapeDtypeStruct((b, t, n), F32)]
    else:
        assert has_delta
        body = functools.partial(_in_odd_kernel, n_ctx)
        in_specs.append(const2((3, d)))
        args.append(conv_w)
        out_specs = [tok, tok]
        out_shape = [jax.ShapeDtypeStruct((b, t, d), F32), jax.ShapeDtypeStruct((b, t, d), BF16)]
    return pl.pallas_call(
        body, grid=(b, t // tm), in_specs=in_specs, out_specs=out_specs, out_shape=out_shape,
        compiler_params=_cparams(("parallel", "parallel")), name="in_proj_" + kind,
    )(*args)


def _prepare_kernel(n_ctx, aw, zc_ref, zp_ref, zn_ref, mup_ref, mun_ref, w0_ref, w2_ref, a0_ref, a2_ref,
                    g2_ref, kk_ref, ka_ref, ones_ref,
                    r_ref, v_ref, kkn_ref, g_ref, ld_ref, kd_ref, icl_ref):
    z = zc_ref[0]
    tm = z.shape[0]
    j = pl.program_id(1)
    nct = n_ctx // tm
    first = jnp.logical_or(j == 0, j == nct)
    last = jnp.logical_or(j == nct - 1, j == pl.num_programs(1) - 1)
    row = lax.broadcasted_iota(jnp.int32, (tm, 1), 0)
    halo_p = jnp.where(first, 0.0, zp_ref[0][7:8])
    halo_n = jnp.where(last, 0.0, zn_ref[0][0:1])
    z_prev = jnp.where(row == 0, halo_p, pltpu.roll(z, 1, axis=0))
    z_next = jnp.where(row == tm - 1, halo_n, pltpu.roll(z, tm - 1, axis=0))
    z = z + mup_ref[...] * (z_prev - z) + mun_ref[...] * (z_next - z)
    r = z[:, 0:aw]
    k = z[:, aw:2 * aw]
    v = z[:, 2 * aw:3 * aw]
    o = 3 * aw
    dw = jnp.tanh(z[:, o:o + 2 * DECAY_LORA])
    da = z[:, o + 2 * DECAY_LORA:o + 2 * DECAY_LORA + 2 * ICL_LORA]
    dg = _sigmoid(z[:, o + 2 * DECAY_LORA + 2 * ICL_LORA:o + 2 * DECAY_LORA + 2 * ICL_LORA + GATE_LORA])
    r_ref[0] = r.astype(BF16)
    v_ref[0] = v.astype(BF16)
    g_ref[0] = _bdot(dg, g2_ref[...]).astype(BF16)
    kk = k * kk_ref[...]
    nrm = jnp.sqrt(_segsum(kk * kk, ones_ref[...]))
    kkn_ref[0] = (kk / jnp.maximum(nrm, 1e-12)).astype(BF16)
    for dr in range(2):
        x = -(w0_ref[dr:dr + 1] + _bdot(dw, w2_ref[dr]))
        softplus = jnp.maximum(x, 0.0) + jnp.log(1.0 + jnp.exp(-jnp.abs(x)))
        ld_ref[dr, 0] = -jnp.exp(-softplus - 0.5)
        icl = _sigmoid(a0_ref[dr:dr + 1] + _bdot(da, a2_ref[dr]))
        icl_ref[dr, 0] = icl.astype(BF16)
        kd_ref[dr, 0] = (k * (1.0 + (icl - 1.0) * ka_ref[...])).astype(BF16)


def _pad_lora(w):
    z = jnp.zeros_like(w[0])
    return jnp.stack([jnp.concatenate([w[0], z], 0), jnp.concatenate([z, w[1]], 0)]).astype(BF16)


def _ones_blockdiag():
    i = jnp.arange(LANE) // A_HEAD
    return (i[:, None] == i[None, :]).astype(BF16)


def _prepare(z, mu_prev, mu_next, w0, w2, a0, a2, g2, k_k, k_a, n_ctx):
    b, t, _ = z.shape
    aw = w0.shape[1]
    ap = mu_prev.shape[0]
    tm = _tile_rows(n_ctx, t - n_ctx)
    hb = tm // 8
    nhb = t // 8
    tok = lambda i, j: (i, j, 0)
    c2 = lambda shape: pl.BlockSpec(shape, lambda i, j: (0,) * len(shape))
    in_specs = [pl.BlockSpec((1, tm, ap), tok),
                pl.BlockSpec((1, 8, ap), lambda i, j: (i, jnp.maximum(j * hb - 1, 0), 0)),
                pl.BlockSpec((1, 8, ap), lambda i, j: (i, jnp.minimum((j + 1) * hb, nhb - 1), 0)),
                c2((1, ap)), c2((1, ap)), c2((2, aw)), c2((2, 2 * DECAY_LORA, aw)), c2((2, aw)),
                c2((2, 2 * ICL_LORA, aw)), c2((GATE_LORA, aw)), c2((1, aw)), c2((1, aw)), c2((LANE, LANE))]
    one = pl.BlockSpec((1, tm, aw), tok)
    two = pl.BlockSpec((2, 1, tm, aw), lambda i, j: (0, i, j, 0))
    s1 = jax.ShapeDtypeStruct((b, t, aw), BF16)
    s2 = jax.ShapeDtypeStruct((2, b, t, aw), BF16)
    return pl.pallas_call(
        functools.partial(_prepare_kernel, n_ctx, aw),
        grid=(b, t // tm), in_specs=in_specs,
        out_specs=[one, one, one, one, two, two, two],
        out_shape=[s1, s1, s1, s1, jax.ShapeDtypeStruct((2, b, t, aw), F32), s2, s2],
        compiler_params=_cparams(("parallel", "parallel")), name="rwkv_prepare",
    )(z, z, z, mu_prev.reshape(1, ap), mu_next.reshape(1, ap), w0, _pad_lora(w2), a0, _pad_lora(a2),
      g2.astype(BF16), k_k.reshape(1, aw), k_a.reshape(1, aw), _ones_blockdiag())


def _scan_kernel(*refs):
    ins, (yf_ref, yb_ref, z_scr) = refs[:12], refs[12:]
    step = pl.program_id(1)
    c = CHUNK

    @pl.when(step == 0)
    def _():
        z_scr[...] = jnp.zeros_like(z_scr)

    ti = lax.broadcasted_iota(jnp.int32, (c, c), 0)
    tj = lax.broadcasted_iota(jnp.int32, (c, c), 1)
    l64 = lax.broadcasted_iota(jnp.int32, (c, PAIR), 1)
    r64 = lax.broadcasted_iota(jnp.int32, (c, PAIR), 0)
    lane_a = l64 < A_HEAD
    li = jnp.where(lane_a, l64, l64 - A_HEAD)
    eye_pl = jnp.where(li == r64, 1.0, 0.0)

    nb = yf_ref.shape[0]
    a_t, r_t, b_t, k_t, b_h, k_h, p_c, vv, before, before_eq = ([] for _ in range(10))
    for bb, d in [(bb, d) for bb in range(nb) for d in range(2)]:
        r_ref, v_ref, kk_ref, ld_ref, kd_ref, icl_ref = ins[6 * d:6 * d + 6]
        fwd = d == 0
        tri = jnp.where((tj <= ti) if fwd else (tj >= ti), 1.0, 0.0).astype(BF16)
        ld = ld_ref[0, bb]
        ld_hi = ld.astype(BF16)
        ld_lo = (ld - ld_hi.astype(F32)).astype(BF16)
        linc = (jnp.dot(tri, ld_hi, preferred_element_type=F32)
                + jnp.dot(tri, ld_lo, preferred_element_type=F32))
        lexc = linc - ld
        ltot = linc[c - 1:c] if fwd else linc[0:1]
        kk = kk_ref[bb].astype(F32)
        bvec = kk * icl_ref[0, bb].astype(F32)
        kd = kd_ref[0, bb].astype(F32)
        e_neg = jnp.exp(-linc)
        e_tot = jnp.exp(ltot - linc)
        a_t.append(-kk * jnp.exp(lexc))
        r_t.append(r_ref[bb].astype(F32) * jnp.exp(linc))
        b_t.append(bvec * e_neg)
        k_t.append(kd * e_neg)
        b_h.append(bvec * e_tot)
        k_h.append(kd * e_tot)
        p_c.append(jnp.exp(ltot))
        vv.append(v_ref[bb].astype(F32))
        bf = (li < r64) if fwd else (li > r64)
        before.append(bf)
        before_eq.append(jnp.logical_or(bf, li == r64))

    r128 = lax.broadcasted_iota(jnp.int32, (PAIR, PAIR), 0)
    l128 = lax.broadcasted_iota(jnp.int32, (PAIR, PAIR), 1)
    diag_blk = (r128 < A_HEAD) == (l128 < A_HEAD)
    eye128 = r128 == l128
    lane_a2 = l128 < A_HEAD

    def bd(x):
        x = x.astype(F32)
        return jnp.where(diag_blk, jnp.concatenate([x, x], axis=0), 0.0).astype(BF16)

    def abd(x):
        x = x.astype(F32)
        return jnp.where(diag_blk, 0.0, jnp.concatenate([x, x], axis=0)).astype(BF16)

    nt = (((1,), (1,)), ((), ()))
    n_pairs = yf_ref.shape[2] // PAIR
    dirs = [k for k in range(2 * nb) for _ in range(n_pairs)]
    sls = [slice(p * PAIR, (p + 1) * PAIR) for _ in range(2 * nb) for p in range(n_pairs)]
    chains = list(zip(dirs, sls))
    mm = lambda a, b: jnp.dot(a.astype(BF16), b, preferred_element_type=F32)
    g1, g2 = [], []
    for d, sl in chains:
        ar = jnp.concatenate([a_t[d][:, sl], r_t[d][:, sl]], axis=0)
        bk = jnp.concatenate([b_t[d][:, sl], k_t[d][:, sl]], axis=0).astype(BF16)
        kb = jnp.concatenate([k_t[d][:, sl], b_t[d][:, sl]], axis=0).astype(BF16)
        g1.append(lax.dot_general(jnp.where(lane_a2, ar, 0.0).astype(BF16), bk, nt, preferred_element_type=F32))
        g2.append(lax.dot_general(jnp.where(lane_a2, 0.0, ar).astype(BF16), kb, nt, preferred_element_type=F32))
    sel_a = [jnp.logical_and(m, lane_a) for m in before]
    sel_a_eq = [jnp.logical_and(m, lane_a) for m in before_eq]
    ab = [jnp.where(sel_a[d], x[:c], jnp.where(before[d], y[:c], 0.0)) for d, x, y in zip(dirs, g1, g2)]
    rb = [jnp.where(sel_a_eq[d], x[c:], jnp.where(before_eq[d], y[c:], 0.0)) for d, x, y in zip(dirs, g1, g2)]
    ak_s = [jnp.where(sel_a[d], y[:c], jnp.where(before[d], x[:c], 0.0)) for d, x, y in zip(dirs, g1, g2)]
    rk_s = [jnp.where(sel_a_eq[d], y[c:], jnp.where(before_eq[d], x[c:], 0.0)) for d, x, y in zip(dirs, g1, g2)]
    akv_rkv = [mm(jnp.concatenate([a, b], axis=0), abd(vv[d][:, sl])) for a, b, (d, sl) in zip(ak_s, rk_s, chains)]
    s = [eye_pl + x for x in ab]
    q = [mm(x, bd(x)) for x in ab]
    n = 2
    while n < c:
        if 2 * n >= c:
            s = [si + mm(qi, bd(si)) for qi, si in zip(q, s)]
        else:
            out = [mm(qi, jnp.concatenate([bd(qi), bd(si)], axis=1)) for qi, si in zip(q, s)]
            s = [si + o[:, PAIR:] for si, o in zip(s, out)]
            q = [o[:, :PAIR] for o in out]
        n *= 2
    ah_w2 = [mm(si, jnp.concatenate([bd(a_t[d][:, sl]), bd(x[:c])], axis=1))
             for si, (d, sl), x in zip(s, chains, akv_rkv)]
    rb_o = [mm(x, jnp.concatenate([bd(o[:, :PAIR]), bd(o[:, PAIR:])], axis=1)) for x, o in zip(rb, ah_w2)]
    pg = []
    for (d, sl), o in zip(chains, ah_w2):
        v_p = vv[d][:, sl]
        bkh_t = jnp.concatenate([b_h[d][:, sl], k_h[d][:, sl]], axis=0).T
        rhs = jnp.concatenate([o, jnp.concatenate([jnp.zeros_like(v_p), v_p], axis=1)], axis=0)
        pg.append(mm(bkh_t, rhs.astype(BF16)))
    ys, zs = [], []
    for i, (d, sl) in enumerate(chains):
        rbar = r_t[d][:, sl] + rb_o[i][:, :PAIR]
        y0 = rb_o[i][:, PAIR:] + akv_rkv[i][c:]
        pc_row = jnp.broadcast_to(p_c[d][:, sl], (PAIR, PAIR))
        phi = jnp.where(diag_blk, pg[i][:, :PAIR], 0.0) + jnp.where(eye128, pc_row, 0.0)
        gam = jnp.where(diag_blk, pg[i][:, PAIR:], 0.0)
        yz = mm(jnp.concatenate([rbar, phi], axis=0), z_scr[i].astype(BF16))
        ys.append(yz[:c] + y0)
        zs.append(yz[c:] + gam)
    for bb in range(nb):
        yf_ref[bb] = jnp.concatenate(ys[2 * bb * n_pairs:(2 * bb + 1) * n_pairs], axis=1)
        yb_ref[bb] = jnp.concatenate(ys[(2 * bb + 1) * n_pairs:(2 * bb + 2) * n_pairs], axis=1)
    for i in range(len(chains)):
        z_scr[i] = zs[i]


def _scan(r, v, kkn, ld, kd, icl, n_ctx):
    b, t, aw = r.shape
    nch = t // CHUNK
    ncc = n_ctx // CHUNK

    def chunk_of(d, i):
        return i if d == 0 else jnp.where(i < ncc, ncc - 1 - i, nch - 1 - (i - ncc))

    nb = 1
    in_specs, args = [], []
    for d in range(2):
        one = pl.BlockSpec((nb, CHUNK, aw), lambda bi, i, d=d: (bi, chunk_of(d, i), 0))
        two = pl.BlockSpec((1, nb, CHUNK, aw), lambda bi, i, d=d: (d, bi, chunk_of(d, i), 0))
        in_specs += [one, one, one, two, two, two]
        args += [r, v, kkn, ld, kd, icl]
    out_specs = [pl.BlockSpec((nb, CHUNK, aw), lambda bi, i, d=d: (bi, chunk_of(d, i), 0)) for d in range(2)]
    return pl.pallas_call(
        _scan_kernel,
        grid=(b // nb, nch),
        in_specs=in_specs,
        out_specs=out_specs,
        out_shape=[jax.ShapeDtypeStruct((b, t, aw), F32)] * 2,
        scratch_shapes=[pltpu.VMEM((2 * nb * (aw // PAIR), PAIR, PAIR), F32)],
        compiler_params=_cparams(("parallel", "arbitrary")), name="wkv7_scan",
    )(*args)


def _conv_kernel(n_ctx, width, u_ref, gate_ref, w_ref, bias_ref, o_ref, lat_scr, ctx_scr):
    half = (width - 1) // 2
    t = u_ref.shape[1]
    seq = t - n_ctx
    padl = half * GRID_W
    padc = 16
    lat_scr[0:padl] = jnp.zeros((padl, LANE), F32)
    lat_scr[padl + seq:] = jnp.zeros((padl, LANE), F32)
    lat_scr[padl:padl + seq] = u_ref[0, n_ctx:] * _sigmoid(gate_ref[0, n_ctx:])
    ctx_scr[0:padc] = jnp.zeros((padc, LANE), F32)
    ctx_scr[padc + n_ctx:] = jnp.zeros((padc, LANE), F32)
    ctx_scr[padc:padc + n_ctx] = u_ref[0, 0:n_ctx] * _sigmoid(gate_ref[0, 0:n_ctx])
    bias = bias_ref[...]
    acc = jnp.zeros((n_ctx, LANE), F32)
    for j in range(width):
        acc = acc + w_ref[j:j + 1] * ctx_scr[padc - half + j:padc - half + j + n_ctx]
    o_ref[0, 0:n_ctx] = acc + bias

    def body(rc, carry):
        acc = jnp.zeros((GRID_W, LANE), F32)
        for j in range(width):
            acc = acc + w_ref[j:j + 1] * lat_scr[pl.ds(pl.multiple_of((rc + j) * GRID_W, GRID_W), GRID_W)]
        o_ref[0, pl.ds(pl.multiple_of(n_ctx + rc * GRID_W, GRID_W), GRID_W)] = acc + bias
        return carry

    lax.fori_loop(0, seq // GRID_W, body, 0)


def _conformer_conv(z, col0, conv_w, conv_bias, n_ctx):
    b, t, _ = z.shape
    width, bw = conv_w.shape
    half = (width - 1) // 2
    nb = bw // LANE
    c0 = col0 // LANE
    return pl.pallas_call(
        functools.partial(_conv_kernel, n_ctx, width),
        grid=(b, nb),
        in_specs=[pl.BlockSpec((1, t, LANE), lambda i, j: (i, 0, c0 + j)),
                  pl.BlockSpec((1, t, LANE), lambda i, j: (i, 0, c0 + nb + j)),
                  pl.BlockSpec((width, LANE), lambda i, j: (0, j)),
                  pl.BlockSpec((1, LANE), lambda i, j: (0, j))],
        out_specs=pl.BlockSpec((1, t, LANE), lambda i, j: (i, 0, j)),
        out_shape=jax.ShapeDtypeStruct((b, t, bw), F32),
        scratch_shapes=[pltpu.VMEM((t - n_ctx + 2 * half * GRID_W, LANE), F32),
                        pltpu.VMEM((n_ctx + 32, LANE), F32)],
        compiler_params=_cparams(("parallel", "parallel")), name="conformer_conv",
    )(z, z, conv_w, conv_bias.reshape(1, bw))


def _merge_kernel(yf_ref, yb_ref, r_ref, v_ref, kd_ref, g_ref, ub_ref, rk_ref, lg_ref, lb_ref, bg_ref, bb_ref,
                  ones_ref, o_ref):
    aw = r_ref.shape[2]
    ones = ones_ref[...]
    y = yf_ref[0] + yb_ref[0]
    inv = 1.0 / A_HEAD
    mean = _segsum(y, ones) * inv
    yc = y - mean
    var = _segsum(yc * yc, ones) * inv
    yn = yc * lax.rsqrt(var + GN_EPS) * lg_ref[...] + lb_ref[...]
    kd_sum = kd_ref[0, 0].astype(F32) + kd_ref[1, 0].astype(F32)
    bonus = _segsum(r_ref[0].astype(F32) * kd_sum * rk_ref[...], ones) * v_ref[0].astype(F32)
    o_ref[0, :, 0:aw] = ((yn + bonus) * g_ref[0].astype(F32)).astype(BF16)
    u = ub_ref[0]
    mu = jnp.mean(u, axis=-1, keepdims=True)
    uc = u - mu
    uv = jnp.mean(uc * uc, axis=-1, keepdims=True)
    un = uc * lax.rsqrt(uv + LN_EPS) * bg_ref[...] + bb_ref[...]
    o_ref[0, :, aw:] = (un * _sigmoid(un)).astype(BF16)


def _merge(y, r, v, kd, g, ub, r_k, lnx_g, lnx_b, lnb_g, lnb_b, n_ctx):
    b, t, aw = r.shape
    bw = ub.shape[2]
    tm = _tile_rows(n_ctx, t - n_ctx)
    one = pl.BlockSpec((1, tm, aw), lambda i, j: (i, j, 0))
    two = pl.BlockSpec((2, 1, tm, aw), lambda i, j: (0, i, j, 0))
    c2 = lambda shape: pl.BlockSpec(shape, lambda i, j: (0, 0))
    return pl.pallas_call(
        _merge_kernel, grid=(b, t // tm),
        in_specs=[one, one, one, one, two, one, pl.BlockSpec((1, tm, bw), lambda i, j: (i, j, 0)),
                  c2((1, aw)), c2((1, aw)), c2((1, aw)), c2((1, bw)), c2((1, bw)), c2((LANE, LANE))],
        out_specs=pl.BlockSpec((1, tm, aw + bw), lambda i, j: (i, j, 0)),
        out_shape=jax.ShapeDtypeStruct((b, t, aw + bw), BF16),
        compiler_params=_cparams(("parallel", "parallel")), name="rwkv_readout_conformer_norm",
    )(*y, r, v, kd, g, ub, r_k.reshape(1, aw), lnx_g.reshape(1, aw), lnx_b.reshape(1, aw),
      lnb_g.reshape(1, bw), lnb_b.reshape(1, bw), _ones_blockdiag())


def _out_kernel(lhs_ref, x_ref, mod_ref, g_ref, w_ref, wr_ref, rb_ref, xo_ref, h_ref, eid_ref, gate_ref):
    m = mod_ref[0, 0]
    x = x_ref[0] + m[2:3] * jnp.dot(lhs_ref[0], w_ref[...], preferred_element_type=F32)
    xo_ref[0] = x
    h = _rms_mod(x, g_ref[...], m[3:4], m[4:5]).astype(BF16)
    h_ref[0] = h
    logits = lax.dot_general(wr_ref[...], h, (((1,), (1,)), ((), ())), preferred_element_type=F32)
    scores = _sigmoid(logits)
    sel = scores + rb_ref[...]
    e = EXPERTS_PER_GROUP
    best = None
    for gi in range(N_GROUPS):
        rows = [sel[gi * e + i:gi * e + i + 1] for i in range(e)]
        top2 = None
        for i in range(e):
            for k in range(i + 1, e):
                s = rows[i] + rows[k]
                top2 = s if top2 is None else jnp.maximum(top2, s)
        if best is None:
            best, best_v = jnp.zeros_like(top2, dtype=jnp.int32), top2
        else:
            upd = top2 > best_v
            best = jnp.where(upd, gi, best)
            best_v = jnp.where(upd, top2, best_v)

    def pick(a, i):
        out = a[i:i + 1]
        for gi in range(1, N_GROUPS):
            out = jnp.where(best == gi, a[gi * e + i:gi * e + i + 1], out)
        return out

    cs = [pick(sel, i) for i in range(e)]
    sc = [pick(scores, i) for i in range(e)]

    def argmax4(vals):
        idx, val = jnp.zeros_like(best), vals[0]
        for i in range(1, e):
            upd = vals[i] > val
            idx = jnp.where(upd, i, idx)
            val = jnp.where(upd, vals[i], val)
        return idx

    i1 = argmax4(cs)
    i2 = argmax4([jnp.where(i1 == i, -jnp.inf, cs[i]) for i in range(e)])

    def take(vals, idx):
        out = vals[0]
        for i in range(1, e):
            out = jnp.where(idx == i, vals[i], out)
        return out

    s1, s2 = take(sc, i1), take(sc, i2)
    tot = s1 + s2
    eid_ref[0] = jnp.concatenate([best * e + i1, best * e + i2], axis=0)
    gate_ref[0] = jnp.concatenate([s1 / tot, s2 / tot], axis=0)


def _out_proj(lhs, x, mod, gain, w, w_router, router_bias, n_ctx):
    b, t, d = x.shape
    tm = _tile_rows(n_ctx, t - n_ctx)
    nct = n_ctx // tm
    k = lhs.shape[2]
    ne = w_router.shape[1]
    tok = lambda i, j: (i, j, 0)
    c2 = lambda shape: pl.BlockSpec(shape, lambda i, j: (0, 0), pipeline_mode=pl.Buffered(1))
    return pl.pallas_call(
        _out_kernel, grid=(b, t // tm),
        in_specs=[pl.BlockSpec((1, tm, k), tok), pl.BlockSpec((1, tm, d), tok),
                  pl.BlockSpec((1, 1, 6, d), lambda i, j: (i, jnp.where(j < nct, 0, 1), 0, 0)),
                  c2((1, d)), c2((k, d)), c2((ne, d)), c2((ne, 1))],
        out_specs=[pl.BlockSpec((1, tm, d), tok), pl.BlockSpec((1, tm, d), tok),
                   pl.BlockSpec((1, 2, tm), lambda i, j: (i, 0, j)),
                   pl.BlockSpec((1, 2, tm), lambda i, j: (i, 0, j))],
        out_shape=[jax.ShapeDtypeStruct((b, t, d), F32), jax.ShapeDtypeStruct((b, t, d), BF16),
                   jax.ShapeDtypeStruct((b, 2, t), jnp.int32), jax.ShapeDtypeStruct((b, 2, t), F32)],
        compiler_params=_cparams(("parallel", "parallel")), name="out_proj_router",
    )(lhs, x, mod, gain.reshape(1, d), w, w_router.T.astype(BF16), router_bias.reshape(ne, 1))


def _expert_kernel(it_ref, ie_ref, fl_ref, lo_ref, hi_ref, ni_ref, x_ref, wg_ref, wu_ref, wd_ref, o_ref,
                   wg_s, wu_s, wd_s):
    j = pl.program_id(0)
    tm = x_ref.shape[0]

    @pl.when(j < ni_ref[0])
    def _():
        @pl.when((fl_ref[j] & 1) != 0)
        def _():
            wg_s[...] = wg_ref[0, 0].astype(BF16)
            wu_s[...] = wu_ref[0, 0].astype(BF16)
            wd_s[...] = wd_ref[0, 0].astype(BF16)

        x = x_ref[...]
        a = jnp.dot(x, wg_s[...], preferred_element_type=F32)
        u = jnp.dot(x, wu_s[...], preferred_element_type=F32)
        he = (a * _sigmoid(a) * u).astype(BF16)
        y = jnp.dot(he, wd_s[...], preferred_element_type=F32).astype(BF16)
        row = it_ref[j] * tm + lax.broadcasted_iota(jnp.int32, (tm, 1), 0)
        mine = jnp.logical_and(row >= lo_ref[j], row < hi_ref[j])

        @pl.when((fl_ref[j] & 2) != 0)
        def _():
            o_ref[...] = jnp.where(mine, y, jnp.zeros_like(y))

        @pl.when((fl_ref[j] & 2) == 0)
        def _():
            o_ref[...] = jnp.where(mine, y, o_ref[...])


def _experts(x_sorted, items, layer, w_gate, w_up, w_down, tm):
    p, d = x_sorted.shape
    f = w_gate.shape[3]
    n_items = items[0].shape[0]
    wspec = lambda shape: pl.BlockSpec((1, 1) + shape, lambda j, it, ie, *_: (layer, ie[j], 0, 0),
                                       pipeline_mode=pl.Buffered(1))
    grid_spec = pltpu.PrefetchScalarGridSpec(
        num_scalar_prefetch=6, grid=(n_items,),
        in_specs=[pl.BlockSpec((tm, d), lambda j, it, *_: (it[j], 0)), wspec((d, f)), wspec((d, f)), wspec((f, d))],
        out_specs=pl.BlockSpec((tm, d), lambda j, it, *_: (it[j], 0)),
        scratch_shapes=[pltpu.VMEM((d, f), BF16), pltpu.VMEM((d, f), BF16), pltpu.VMEM((f, d), BF16)])
    return pl.pallas_call(
        _expert_kernel, grid_spec=grid_spec,
        out_shape=jax.ShapeDtypeStruct((p, d), BF16),
        compiler_params=_cparams(("arbitrary",)), name="moe_experts",
    )(*items, x_sorted, w_gate, w_up, w_down)


def _moe(h2, eid, gates, layer, w_gate, w_up, w_down, tm):
    b, t, d = h2.shape
    n = b * t
    ne = w_gate.shape[1]
    e_flat = eid.transpose(0, 2, 1).reshape(n * 2)
    slot = jnp.arange(2 * n, dtype=jnp.int32)
    sorted_e, order = lax.sort((e_flat, slot), num_keys=1, is_stable=True)
    _, pos = lax.sort((order, slot), num_keys=1)
    bounds = jnp.sum(sorted_e[None, :] < jnp.arange(ne + 1, dtype=jnp.int32)[:, None], axis=1).astype(jnp.int32)
    lo_t = bounds[:-1] // tm
    n_e = jnp.where(bounds[1:] > bounds[:-1], (bounds[1:] - 1) // tm - lo_t + 1, 0)
    item_end = jnp.cumsum(n_e)
    item_start = item_end - n_e
    n_items = item_end[-1]
    max_items = (2 * n) // tm + ne - 1
    jj = jnp.minimum(jnp.arange(max_items, dtype=jnp.int32), n_items - 1)
    ie = jnp.minimum(jnp.sum(item_end[None, :] <= jj[:, None], axis=1), ne - 1).astype(jnp.int32)
    onehot = ie[:, None] == jnp.arange(ne, dtype=jnp.int32)[None, :]
    pick = lambda tab: jnp.sum(jnp.where(onehot, tab[None, :], 0), axis=1).astype(jnp.int32)
    it = pick(lo_t) + jj - pick(item_start)
    first = jnp.arange(max_items) == 0
    new_e = jnp.logical_or(first, ie != jnp.roll(ie, 1))
    new_t = jnp.logical_or(first, it != jnp.roll(it, 1))
    flags = new_e.astype(jnp.int32) + 2 * new_t.astype(jnp.int32)
    items = (it, ie, flags, pick(bounds[:-1]), pick(bounds[1:]), n_items.reshape(1).astype(jnp.int32))
    x_sorted = jnp.take(h2.reshape(n, d), order // 2, axis=0, mode="clip")
    y_sorted = _experts(x_sorted, items, layer, w_gate, w_up, w_down, tm)
    y_pair = jnp.take(y_sorted, pos, axis=0, mode="clip").reshape(b, t, 2 * d)
    return y_pair, gates.transpose(0, 2, 1)


def _final_kernel(x_ref, y_ref, gt_ref, mprev_ref, g_ref, o_ref):
    x = _residual_in(x_ref, (y_ref, gt_ref, mprev_ref))
    o_ref[0] = x * lax.rsqrt(jnp.mean(x * x, axis=-1, keepdims=True) + RMS_EPS) * g_ref[...]


def _final(x, delta, mod_prev, gain, n_ctx):
    b, t, d = x.shape
    tm = _tile_rows(n_ctx, t - n_ctx)
    nct = n_ctx // tm
    tok = pl.BlockSpec((1, tm, d), lambda i, j: (i, j + nct, 0))
    return pl.pallas_call(
        _final_kernel, grid=(b, (t - n_ctx) // tm),
        in_specs=[tok, pl.BlockSpec((1, tm, 2 * d), lambda i, j: (i, j + nct, 0)),
                  pl.BlockSpec((1, tm, 2), lambda i, j: (i, j + nct, 0)),
                  pl.BlockSpec((1, 1, 6, d), lambda i, j: (i, 1, 0, 0)),
                  pl.BlockSpec((1, d), lambda i, j: (0, 0))],
        out_specs=pl.BlockSpec((1, tm, d), lambda i, j: (i, j, 0)),
        out_shape=jax.ShapeDtypeStruct((b, t - n_ctx, d), F32),
        compiler_params=_cparams(("parallel", "parallel")), name="final_norm",
    )(x, *delta, mod_prev, gain.reshape(1, d))


def kernel(x, c, ctx, c_ctx, norm1_g, norm2_g, w_mod, b_mod, w_in_e, mu_prev, mu_next, w0, w2, a0, a2, g2,
           k_k, k_a, r_k, lnx_g, lnx_b, conv_b_w, conv_b_bias, lnb_g, lnb_b, w_out_e, w_in_o, conv_c_w,
           w_out_o, w_router, router_bias, w_gate, w_up, w_down, final_g):
    b, s, d = x.shape
    n_ctx = ctx.shape[1]
    depth = w_mod.shape[0]
    tm = _tile_rows(n_ctx, s)
    aw = w0.shape[2]
    a_proj = mu_prev.shape[1]

    xa = (ctx, x)
    rows = ((b + 1 + 7) // 8) * 8
    cv = jnp.zeros((rows, d), F32).at[:b].set(c).at[b].set(c_ctx)
    mods_all = _adaln(cv, w_mod, b_mod).reshape(depth, rows, 6, d)
    mods = [jnp.stack([jnp.broadcast_to(mods_all[i, b], (b, 6, d)), mods_all[i, :b]], axis=1)
            for i in range(depth)]

    delta = None
    for i in range(depth):
        j = i // 2
        mod_prev = mods[i - 1] if i else None
        if i % 2 == 0:
            xa, z = _in_proj("even", xa, delta, mod_prev, mods[i], norm1_g[i], w_in_e[j].astype(BF16), None,
                             n_ctx)
            r, v, kkn, g, ld, kd, icl = _prepare(z, mu_prev[j], mu_next[j], w0[j], w2[j], a0[j], a2[j], g2[j],
                                                 k_k[j], k_a[j], n_ctx)
            y = _scan(r, v, kkn, ld, kd, icl, n_ctx)
            ub = _conformer_conv(z, a_proj, conv_b_w[j], conv_b_bias[j], n_ctx)
            lhs = _merge(y, r, v, kd, g, ub, r_k[j].reshape(aw), lnx_g[j], lnx_b[j], lnb_g[j], lnb_b[j], n_ctx)
            w_out = w_out_e[j]
        else:
            xa, lhs = _in_proj("odd", xa, delta, mod_prev, mods[i], norm1_g[i], w_in_o[j].astype(BF16),
                               conv_c_w[j], n_ctx)
            w_out = w_out_o[j]
        xa, h2, eid, gates = _out_proj(lhs, xa, mods[i], norm2_g[i], w_out.astype(BF16), w_router, router_bias,
                                       n_ctx)
        delta = _moe(h2, eid, gates, i, w_gate, w_up, w_down, tm)
    return _final(xa, delta, mods[depth - 1], final_g, n_ctx)
```

```python
import functools

import jax
import jax.numpy as jnp
from jax import lax
from jax.experimental import pallas as pl
from jax.experimental.pallas import tpu as pltpu

F32 = jnp.float32
BF16 = jnp.bfloat16

GRID_W = 64
A_HEAD = 64
DECAY_LORA = 64
ICL_LORA = 64
GATE_LORA = 128
N_EXPERTS = 16
N_GROUPS = 4
EXPERTS_PER_GROUP = N_EXPERTS // N_GROUPS
RMS_EPS = 1e-6
LN_EPS = 1e-5
GN_EPS = 64e-5

LANE = 128
CHUNK = 64
PAIR = 2 * A_HEAD
VMEM_LIMIT = 56 * 1024 * 1024


def _cparams(sem):
    return pltpu.CompilerParams(dimension_semantics=sem, vmem_limit_bytes=VMEM_LIMIT)


def _sigmoid(x):
    return 1.0 / (1.0 + jnp.exp(-x))


def _bdot(a, b):
    return jnp.dot(a.astype(BF16), b.astype(BF16), preferred_element_type=F32)


def _rms_mod(x, g, shift, scale):
    y = x * lax.rsqrt(jnp.mean(x * x, axis=-1, keepdims=True) + RMS_EPS)
    return (y * g) * (1.0 + scale) + shift


def _segsum(x, ones_bd):
    outs = []
    for c in range(x.shape[1] // LANE):
        xs = x[:, c * LANE:(c + 1) * LANE]
        hi = xs.astype(BF16)
        lo = (xs - hi.astype(F32)).astype(BF16)
        outs.append(jnp.dot(hi, ones_bd, preferred_element_type=F32)
                    + jnp.dot(lo, ones_bd, preferred_element_type=F32))
    return jnp.concatenate(outs, axis=1)


def _adaln_kernel(cv_ref, w_ref, b_ref, o_ref):
    cv = cv_ref[...]
    s = cv * _sigmoid(cv)
    o_ref[0] = _bdot(s, w_ref[0]) + b_ref[0]


def _adaln(cv, w_mod, b_mod):
    depth, d, n = w_mod.shape
    rows = cv.shape[0]
    tn = 1024
    return pl.pallas_call(
        _adaln_kernel,
        grid=(depth, n // tn),
        in_specs=[pl.BlockSpec((rows, d), lambda i, j: (0, 0)),
                  pl.BlockSpec((1, d, tn), lambda i, j: (i, 0, j)),
                  pl.BlockSpec((1, 1, tn), lambda i, j: (i, 0, j))],
        out_specs=pl.BlockSpec((1, rows, tn), lambda i, j: (i, 0, j)),
        out_shape=jax.ShapeDtypeStruct((depth, rows, n), F32),
        compiler_params=_cparams(("parallel", "parallel")),
        name="adaln",
    )(cv, w_mod, b_mod.reshape(depth, 1, n))


def _residual_in(x_ref, moe_refs):
    x = x_ref[0]
    if moe_refs is not None:
        y1_ref, y2_ref, gt_ref, mprev_ref = moe_refs
        gt = gt_ref[0]
        moe = gt[:, 0:1] * y1_ref[0].astype(F32) + gt[:, 1:2] * y2_ref[0].astype(F32)
        x = x + mprev_ref[0, 0][5:6] * moe
    return x


def _in_even_kernel(has_delta, nct, *refs):
    if has_delta:
        x = _residual_in(refs[0], refs[1:5])
        refs = refs[5:]
    else:
        x = jnp.where(pl.program_id(1) < nct, refs[0][0], refs[1][0])
        refs = refs[2:]
    mod_ref, g_ref, w_ref, xo_ref, z_ref = refs
    xo_ref[0] = x
    m = mod_ref[0, 0]
    h = _rms_mod(x, g_ref[...], m[0:1], m[1:2])
    z_ref[0] = jnp.dot(h.astype(BF16), w_ref[...], preferred_element_type=F32)


def _in_odd_kernel(n_ctx, *refs):
    x = _residual_in(refs[0], refs[1:5])
    mod_ref, g_ref, w_ref, cw_ref, xo_ref, v_ref = refs[5:]
    xo_ref[0] = x
    m = mod_ref[0, 0]
    h = _rms_mod(x, g_ref[...], m[0:1], m[1:2])
    z = jnp.dot(h.astype(BF16), w_ref[...], preferred_element_type=F32)
    tm, d = x.shape
    bg = z[:, 0:d]
    u = z[:, d:2 * d] * z[:, 2 * d:3 * d]
    pos = pl.program_id(1) * tm + lax.broadcasted_iota(jnp.int32, (tm, 1), 0)
    is_ctx = pos < n_ctx
    col = lax.rem(pos - n_ctx, GRID_W)
    prev_ok = jnp.where(is_ctx, pos, col) != 0
    next_ok = jnp.where(is_ctx, pos - (n_ctx - 1), col - (GRID_W - 1)) != 0
    u_prev = jnp.where(prev_ok, pltpu.roll(u, 1, axis=0), 0.0)
    u_next = jnp.where(next_ok, pltpu.roll(u, tm - 1, axis=0), 0.0)
    cw = cw_ref[...]
    conv = cw[0:1] * u_prev + cw[1:2] * u + cw[2:3] * u_next
    v_ref[0] = (bg * conv).astype(BF16)


def _tile_rows(n_ctx, seq):
    tm = 256
    while n_ctx % tm or seq % tm:
        tm //= 2
    assert tm % GRID_W == 0 and tm % CHUNK == 0
    return tm


def _in_proj(kind, x, delta, mod_prev, mod, gain, w, conv_w, n_ctx):
    has_delta = delta is not None
    if has_delta:
        b, t, d = x.shape
    else:
        b, t, d = x[0].shape[0], x[0].shape[1] + x[1].shape[1], x[0].shape[2]
    tm = _tile_rows(n_ctx, t - n_ctx)
    nct = n_ctx // tm
    n = w.shape[1]
    tok = pl.BlockSpec((1, tm, d), lambda i, j: (i, j, 0))
    modspec = pl.BlockSpec((1, 1, 6, d), lambda i, j: (i, jnp.where(j < nct, 0, 1), 0, 0))
    const2 = lambda shape: pl.BlockSpec(shape, lambda i, j: (0, 0), pipeline_mode=pl.Buffered(1))
    if has_delta:
        in_specs = [tok, tok, tok, pl.BlockSpec((1, tm, 2), lambda i, j: (i, j, 0)), modspec]
        args = [x, *delta, mod_prev]
    else:
        in_specs = [pl.BlockSpec((1, tm, d), lambda i, j: (i, jnp.minimum(j, nct - 1), 0)),
                    pl.BlockSpec((1, tm, d), lambda i, j: (i, jnp.maximum(j - nct, 0), 0))]
        args = [*x]
    in_specs += [modspec, const2((1, d)), const2((d, n))]
    args += [mod, gain.reshape(1, d), w]
    if kind == "even":
        body = functools.partial(_in_even_kernel, has_delta, nct)
        out_specs = [tok, pl.BlockSpec((1, tm, n), lambda i, j: (i, j, 0))]
        out_shape = [jax.ShapeDtypeStruct((b, t, d), F32), jax.ShapeDtypeStruct((b, t, n), F32)]
    else:
        assert has_delta
        body = functools.partial(_in_odd_kernel, n_ctx)
        in_specs.append(const2((3, d)))
        args.append(conv_w)
        out_specs = [tok, tok]
        out_shape = [jax.ShapeDtypeStruct((b, t, d), F32), jax.ShapeDtypeStruct((b, t, d), BF16)]
    return pl.pallas_call(
        body, grid=(b, t // tm), in_specs=in_specs, out_specs=out_specs, out_shape=out_shape,
        compiler_params=_cparams(("parallel", "parallel")), name="in_proj_" + kind,
    )(*args)


def _prepare_kernel(n_ctx, aw, zc_ref, zp_ref, zn_ref, mup_ref, mun_ref, w0_ref, w2_ref, a0_ref, a2_ref,
                    g2_ref, kk_ref, ka_ref, ones_ref,
                    r_ref, v_ref, kkn_ref, g_ref, ld_ref, kd_ref, icl_ref):
    z = zc_ref[0]
    tm = z.shape[0]
    j = pl.program_id(1)
    nct = n_ctx // tm
    first = jnp.logical_or(j == 0, j == nct)
    last = jnp.logical_or(j == nct - 1, j == pl.num_programs(1) - 1)
    row = lax.broadcasted_iota(jnp.int32, (tm, 1), 0)
    halo_p = jnp.where(first, 0.0, zp_ref[0][7:8])
    halo_n = jnp.where(last, 0.0, zn_ref[0][0:1])
    z_prev = jnp.where(row == 0, halo_p, pltpu.roll(z, 1, axis=0))
    z_next = jnp.where(row == tm - 1, halo_n, pltpu.roll(z, tm - 1, axis=0))
    z = (1.0 - mup_ref[...] - mun_ref[...]) * z + mup_ref[...] * z_prev + mun_ref[...] * z_next
    r = z[:, 0:aw]
    k = z[:, aw:2 * aw]
    v = z[:, 2 * aw:3 * aw]
    o = 3 * aw
    dw = jnp.tanh(z[:, o:o + 2 * DECAY_LORA])
    da = z[:, o + 2 * DECAY_LORA:o + 2 * DECAY_LORA + 2 * ICL_LORA]
    dg = _sigmoid(z[:, o + 2 * DECAY_LORA + 2 * ICL_LORA:o + 2 * DECAY_LORA + 2 * ICL_LORA + GATE_LORA])
    r_ref[0] = r.astype(BF16)
    v_ref[0] = v.astype(BF16)
    g_ref[0] = _bdot(dg, g2_ref[...]).astype(BF16)
    kk = k * kk_ref[...]
    nrm = jnp.sqrt(_segsum(kk * kk, ones_ref[...]))
    kkn_ref[0] = (kk / jnp.maximum(nrm, 1e-12)).astype(BF16)
    for dr in range(2):
        x = -(w0_ref[dr:dr + 1] + _bdot(dw, w2_ref[dr]))
        softplus = jnp.maximum(x, 0.0) + jnp.log(1.0 + jnp.exp(-jnp.abs(x)))
        ld_ref[dr, 0] = -jnp.exp(-softplus - 0.5)
        icl = _sigmoid(a0_ref[dr:dr + 1] + _bdot(da, a2_ref[dr]))
        icl_ref[dr, 0] = icl.astype(BF16)
        kd_ref[dr, 0] = (k * (1.0 + (icl - 1.0) * ka_ref[...])).astype(BF16)


def _pad_lora(w):
    z = jnp.zeros_like(w[0])
    return jnp.stack([jnp.concatenate([w[0], z], 0), jnp.concatenate([z, w[1]], 0)]).astype(BF16)


def _ones_blockdiag():
    i = jnp.arange(LANE) // A_HEAD
    return (i[:, None] == i[None, :]).astype(BF16)


def _prepare(z, mu_prev, mu_next, w0, w2, a0, a2, g2, k_k, k_a, n_ctx):
    b, t, _ = z.shape
    aw = w0.shape[1]
    ap = mu_prev.shape[0]
    tm = _tile_rows(n_ctx, t - n_ctx)
    hb = tm // 8
    nhb = t // 8
    tok = lambda i, j: (i, j, 0)
    c2 = lambda shape: pl.BlockSpec(shape, lambda i, j: (0,) * len(shape))
    in_specs = [pl.BlockSpec((1, tm, ap), tok),
                pl.BlockSpec((1, 8, ap), lambda i, j: (i, jnp.maximum(j * hb - 1, 0), 0)),
                pl.BlockSpec((1, 8, ap), lambda i, j: (i, jnp.minimum((j + 1) * hb, nhb - 1), 0)),
                c2((1, ap)), c2((1, ap)), c2((2, aw)), c2((2, 2 * DECAY_LORA, aw)), c2((2, aw)),
                c2((2, 2 * ICL_LORA, aw)), c2((GATE_LORA, aw)), c2((1, aw)), c2((1, aw)), c2((LANE, LANE))]
    one = pl.BlockSpec((1, tm, aw), tok)
    two = pl.BlockSpec((2, 1, tm, aw), lambda i, j: (0, i, j, 0))
    s1 = jax.ShapeDtypeStruct((b, t, aw), BF16)
    s2 = jax.ShapeDtypeStruct((2, b, t, aw), BF16)
    return pl.pallas_call(
        functools.partial(_prepare_kernel, n_ctx, aw),
        grid=(b, t // tm), in_specs=in_specs,
        out_specs=[one, one, one, one, two, two, two],
        out_shape=[s1, s1, s1, s1, jax.ShapeDtypeStruct((2, b, t, aw), F32), s2, s2],
        compiler_params=_cparams(("parallel", "parallel")), name="rwkv_prepare",
    )(z, z, z, mu_prev.reshape(1, ap), mu_next.reshape(1, ap), w0, _pad_lora(w2), a0, _pad_lora(a2),
      g2.astype(BF16), k_k.reshape(1, aw), k_a.reshape(1, aw), _ones_blockdiag())


def _scan_kernel(*refs):
    ins, (yf_ref, yb_ref, z_scr) = refs[:12], refs[12:]
    step = pl.program_id(1)
    c = CHUNK

    @pl.when(step == 0)
    def _():
        z_scr[...] = jnp.zeros_like(z_scr)

    ti = lax.broadcasted_iota(jnp.int32, (c, c), 0)
    tj = lax.broadcasted_iota(jnp.int32, (c, c), 1)
    l64 = lax.broadcasted_iota(jnp.int32, (c, PAIR), 1)
    r64 = lax.broadcasted_iota(jnp.int32, (c, PAIR), 0)
    lane_a = l64 < A_HEAD
    li = jnp.where(lane_a, l64, l64 - A_HEAD)
    eye_pl = jnp.where(li == r64, 1.0, 0.0)

    nb = yf_ref.shape[0]
    a_t, r_t, b_t, k_t, b_h, k_h, p_c, vv, before, before_eq = ([] for _ in range(10))
    for bb, d in [(bb, d) for bb in range(nb) for d in range(2)]:
        r_ref, v_ref, kk_ref, ld_ref, kd_ref, icl_ref = ins[6 * d:6 * d + 6]
        fwd = d == 0
        tri = jnp.where((tj <= ti) if fwd else (tj >= ti), 1.0, 0.0).astype(BF16)
        ld = ld_ref[0, bb]
        ld_hi = ld.astype(BF16)
        ld_lo = (ld - ld_hi.astype(F32)).astype(BF16)
        linc = (jnp.dot(tri, ld_hi, preferred_element_type=F32)
                + jnp.dot(tri, ld_lo, preferred_element_type=F32))
        lexc = linc - ld
        ltot = linc[c - 1:c] if fwd else linc[0:1]
        kk = kk_ref[bb].astype(F32)
        bvec = kk * icl_ref[0, bb].astype(F32)
        kd = kd_ref[0, bb].astype(F32)
        e_neg = jnp.exp(-linc)
        e_tot = jnp.exp(ltot - linc)
        a_t.append(-kk * jnp.exp(lexc))
        r_t.append(r_ref[bb].astype(F32) * jnp.exp(linc))
        b_t.append(bvec * e_neg)
        k_t.append(kd * e_neg)
        b_h.append(bvec * e_tot)
        k_h.append(kd * e_tot)
        p_c.append(jnp.exp(ltot))
        vv.append(v_ref[bb].astype(F32))
        bf = (li < r64) if fwd else (li > r64)
        before.append(bf)
        before_eq.append(jnp.logical_or(bf, li == r64))

    r128 = lax.broadcasted_iota(jnp.int32, (PAIR, PAIR), 0)
    l128 = lax.broadcasted_iota(jnp.int32, (PAIR, PAIR), 1)
    diag_blk = (r128 < A_HEAD) == (l128 < A_HEAD)
    eye128 = r128 == l128
    lane_a2 = l128 < A_HEAD

    def bd(x):
        x = x.astype(F32)
        return jnp.where(diag_blk, jnp.concatenate([x, x], axis=0), 0.0).astype(BF16)

    def abd(x):
        x = x.astype(F32)
        return jnp.where(diag_blk, 0.0, jnp.concatenate([x, x], axis=0)).astype(BF16)

    nt = (((1,), (1,)), ((), ()))
    n_pairs = yf_ref.shape[2] // PAIR
    dirs = [k for k in range(2 * nb) for _ in range(n_pairs)]
    sls = [slice(p * PAIR, (p + 1) * PAIR) for _ in range(2 * nb) for p in range(n_pairs)]
    chains = list(zip(dirs, sls))
    mm = lambda a, b: jnp.dot(a.astype(BF16), b, preferred_element_type=F32)
    g1, g2 = [], []
    for d, sl in chains:
        ar = jnp.concatenate([a_t[d][:, sl], r_t[d][:, sl]], axis=0)
        bk = jnp.concatenate([b_t[d][:, sl], k_t[d][:, sl]], axis=0).astype(BF16)
        kb = jnp.concatenate([k_t[d][:, sl], b_t[d][:, sl]], axis=0).astype(BF16)
        g1.append(lax.dot_general(jnp.where(lane_a2, ar, 0.0).astype(BF16), bk, nt, preferred_element_type=F32))
        g2.append(lax.dot_general(jnp.where(lane_a2, 0.0, ar).astype(BF16), kb, nt, preferred_element_type=F32))
    sel_a = [jnp.logical_and(m, lane_a) for m in before]
    sel_a_eq = [jnp.logical_and(m, lane_a) for m in before_eq]
    ab = [jnp.where(sel_a[d], x[:c], jnp.where(before[d], y[:c], 0.0)) for d, x, y in zip(dirs, g1, g2)]
    rb = [jnp.where(sel_a_eq[d], x[c:], jnp.where(before_eq[d], y[c:], 0.0)) for d, x, y in zip(dirs, g1, g2)]
    ak_s = [jnp.where(sel_a[d], y[:c], jnp.where(before[d], x[:c], 0.0)) for d, x, y in zip(dirs, g1, g2)]
    rk_s = [jnp.where(sel_a_eq[d], y[c:], jnp.where(before_eq[d], x[c:], 0.0)) for d, x, y in zip(dirs, g1, g2)]
    akv_rkv = [mm(jnp.concatenate([a, b], axis=0), abd(vv[d][:, sl])) for a, b, (d, sl) in zip(ak_s, rk_s, chains)]
    s = [eye_pl + x for x in ab]
    q = [mm(x, bd(x)) for x in ab]
    n = 2
    while n < c:
        if 2 * n >= c:
            s = [si + mm(qi, bd(si)) for qi, si in zip(q, s)]
        else:
            out = [mm(qi, jnp.concatenate([bd(qi), bd(si)], axis=1)) for qi, si in zip(q, s)]
            s = [si + o[:, PAIR:] for si, o in zip(s, out)]
            q = [o[:, :PAIR] for o in out]
        n *= 2
    ah_w2 = [mm(si, jnp.concatenate([bd(a_t[d][:, sl]), bd(x[:c])], axis=1))
             for si, (d, sl), x in zip(s, chains, akv_rkv)]
    rb_o = [mm(x, jnp.concatenate([bd(o[:, :PAIR]), bd(o[:, PAIR:])], axis=1)) for x, o in zip(rb, ah_w2)]
    pg = []
    for (d, sl), o in zip(chains, ah_w2):
        v_p = vv[d][:, sl]
        bkh_t = jnp.concatenate([b_h[d][:, sl], k_h[d][:, sl]], axis=0).T
        rhs = jnp.concatenate([o, jnp.concatenate([jnp.zeros_like(v_p), v_p], axis=1)], axis=0)
        pg.append(mm(bkh_t, rhs.astype(BF16)))
    ys, zs = [], []
    for i, (d, sl) in enumerate(chains):
        rbar = r_t[d][:, sl] + rb_o[i][:, :PAIR]
        y0 = rb_o[i][:, PAIR:] + akv_rkv[i][c:]
        pc_row = jnp.broadcast_to(p_c[d][:, sl], (PAIR, PAIR))
        phi = jnp.where(diag_blk, pg[i][:, :PAIR], 0.0) + jnp.where(eye128, pc_row, 0.0)
        gam = jnp.where(diag_blk, pg[i][:, PAIR:], 0.0)
        yz = mm(jnp.concatenate([rbar, phi], axis=0), z_scr[i].astype(BF16))
        ys.append(yz[:c] + y0)
        zs.append(yz[c:] + gam)
    for bb in range(nb):
        yf_ref[bb] = jnp.concatenate(ys[2 * bb * n_pairs:(2 * bb + 1) * n_pairs], axis=1)
        yb_ref[bb] = jnp.concatenate(ys[(2 * bb + 1) * n_pairs:(2 * bb + 2) * n_pairs], axis=1)
    for i in range(len(chains)):
        z_scr[i] = zs[i]


def _scan(r, v, kkn, ld, kd, icl, n_ctx):
    b, t, aw = r.shape
    nch = t // CHUNK
    ncc = n_ctx // CHUNK

    def chunk_of(d, i):
        return i if d == 0 else jnp.where(i < ncc, ncc - 1 - i, nch - 1 - (i - ncc))

    nb = 1
    in_specs, args = [], []
    for d in range(2):
        one = pl.BlockSpec((nb, CHUNK, aw), lambda bi, i, d=d: (bi, chunk_of(d, i), 0))
        two = pl.BlockSpec((1, nb, CHUNK, aw), lambda bi, i, d=d: (d, bi, chunk_of(d, i), 0))
        in_specs += [one, one, one, two, two, two]
        args += [r, v, kkn, ld, kd, icl]
    out_specs = [pl.BlockSpec((nb, CHUNK, aw), lambda bi, i, d=d: (bi, chunk_of(d, i), 0)) for d in range(2)]
    return pl.pallas_call(
        _scan_kernel,
        grid=(b // nb, nch),
        in_specs=in_specs,
        out_specs=out_specs,
        out_shape=[jax.ShapeDtypeStruct((b, t, aw), F32)] * 2,
        scratch_shapes=[pltpu.VMEM((2 * nb * (aw // PAIR), PAIR, PAIR), F32)],
        compiler_params=_cparams(("parallel", "arbitrary")), name="wkv7_scan",
    )(*args)


def _conv_kernel(n_ctx, width, u_ref, gate_ref, w_ref, bias_ref, o_ref, lat_scr, ctx_scr):
    half = (width - 1) // 2
    t = u_ref.shape[1]
    seq = t - n_ctx
    padl = half * GRID_W
    padc = 16
    lat_scr[0:padl] = jnp.zeros((padl, LANE), F32)
    lat_scr[padl + seq:] = jnp.zeros((padl, LANE), F32)
    lat_scr[padl:padl + seq] = u_ref[0, n_ctx:] * _sigmoid(gate_ref[0, n_ctx:])
    ctx_scr[0:padc] = jnp.zeros((padc, LANE), F32)
    ctx_scr[padc + n_ctx:] = jnp.zeros((padc, LANE), F32)
    ctx_scr[padc:padc + n_ctx] = u_ref[0, 0:n_ctx] * _sigmoid(gate_ref[0, 0:n_ctx])
    bias = bias_ref[...]
    acc = jnp.zeros((n_ctx, LANE), F32)
    for j in range(width):
        acc = acc + w_ref[j:j + 1] * ctx_scr[padc - half + j:padc - half + j + n_ctx]
    o_ref[0, 0:n_ctx] = acc + bias

    def body(rc, carry):
        acc = jnp.zeros((GRID_W, LANE), F32)
        for j in range(width):
            acc = acc + w_ref[j:j + 1] * lat_scr[pl.ds(pl.multiple_of((rc + j) * GRID_W, GRID_W), GRID_W)]
        o_ref[0, pl.ds(pl.multiple_of(n_ctx + rc * GRID_W, GRID_W), GRID_W)] = acc + bias
        return carry

    lax.fori_loop(0, seq // GRID_W, body, 0)


def _conformer_conv(z, col0, conv_w, conv_bias, n_ctx):
    b, t, _ = z.shape
    width, bw = conv_w.shape
    half = (width - 1) // 2
    nb = bw // LANE
    c0 = col0 // LANE
    return pl.pallas_call(
        functools.partial(_conv_kernel, n_ctx, width),
        grid=(b, nb),
        in_specs=[pl.BlockSpec((1, t, LANE), lambda i, j: (i, 0, c0 + j)),
                  pl.BlockSpec((1, t, LANE), lambda i, j: (i, 0, c0 + nb + j)),
                  pl.BlockSpec((width, LANE), lambda i, j: (0, j)),
                  pl.BlockSpec((1, LANE), lambda i, j: (0, j))],
        out_specs=pl.BlockSpec((1, t, LANE), lambda i, j: (i, 0, j)),
        out_shape=jax.ShapeDtypeStruct((b, t, bw), F32),
        scratch_shapes=[pltpu.VMEM((t - n_ctx + 2 * half * GRID_W, LANE), F32),
                        pltpu.VMEM((n_ctx + 32, LANE), F32)],
        compiler_params=_cparams(("parallel", "parallel")), name="conformer_conv",
    )(z, z, conv_w, conv_bias.reshape(1, bw))


def _merge_kernel(yf_ref, yb_ref, r_ref, v_ref, kd_ref, g_ref, ub_ref, rk_ref, lg_ref, lb_ref, bg_ref, bb_ref,
                  ones_ref, o_ref):
    aw = r_ref.shape[2]
    ones = ones_ref[...]
    y = yf_ref[0] + yb_ref[0]
    inv = 1.0 / A_HEAD
    mean = _segsum(y, ones) * inv
    yc = y - mean
    var = _segsum(yc * yc, ones) * inv
    yn = yc * lax.rsqrt(var + GN_EPS) * lg_ref[...] + lb_ref[...]
    kd_sum = kd_ref[0, 0].astype(F32) + kd_ref[1, 0].astype(F32)
    bonus = _segsum(r_ref[0].astype(F32) * kd_sum * rk_ref[...], ones) * v_ref[0].astype(F32)
    o_ref[0, :, 0:aw] = ((yn + bonus) * g_ref[0].astype(F32)).astype(BF16)
    u = ub_ref[0]
    mu = jnp.mean(u, axis=-1, keepdims=True)
    uc = u - mu
    uv = jnp.mean(uc * uc, axis=-1, keepdims=True)
    un = uc * lax.rsqrt(uv + LN_EPS) * bg_ref[...] + bb_ref[...]
    o_ref[0, :, aw:] = (un * _sigmoid(un)).astype(BF16)


def _merge(y, r, v, kd, g, ub, r_k, lnx_g, lnx_b, lnb_g, lnb_b, n_ctx):
    b, t, aw = r.shape
    bw = ub.shape[2]
    tm = _tile_rows(n_ctx, t - n_ctx)
    one = pl.BlockSpec((1, tm, aw), lambda i, j: (i, j, 0))
    two = pl.BlockSpec((2, 1, tm, aw), lambda i, j: (0, i, j, 0))
    c2 = lambda shape: pl.BlockSpec(shape, lambda i, j: (0, 0))
    return pl.pallas_call(
        _merge_kernel, grid=(b, t // tm),
        in_specs=[one, one, one, one, two, one, pl.BlockSpec((1, tm, bw), lambda i, j: (i, j, 0)),
                  c2((1, aw)), c2((1, aw)), c2((1, aw)), c2((1, bw)), c2((1, bw)), c2((LANE, LANE))],
        out_specs=pl.BlockSpec((1, tm, aw + bw), lambda i, j: (i, j, 0)),
        out_shape=jax.ShapeDtypeStruct((b, t, aw + bw), BF16),
        compiler_params=_cparams(("parallel", "parallel")), name="rwkv_readout_conformer_norm",
    )(*y, r, v, kd, g, ub, r_k.reshape(1, aw), lnx_g.reshape(1, aw), lnx_b.reshape(1, aw),
      lnb_g.reshape(1, bw), lnb_b.reshape(1, bw), _ones_blockdiag())


def _out_kernel(lhs_ref, x_ref, mod_ref, g_ref, w_ref, wr_ref, rb_ref, xo_ref, h_ref, eid_ref, gate_ref):
    m = mod_ref[0, 0]
    x = x_ref[0] + m[2:3] * jnp.dot(lhs_ref[0], w_ref[...], preferred_element_type=F32)
    xo_ref[0] = x
    h = _rms_mod(x, g_ref[...], m[3:4], m[4:5]).astype(BF16)
    h_ref[0] = h
    logits = lax.dot_general(wr_ref[...], h, (((1,), (1,)), ((), ())), preferred_element_type=F32)
    scores = _sigmoid(logits)
    sel = scores + rb_ref[...]
    e = EXPERTS_PER_GROUP
    best = None
    for gi in range(N_GROUPS):
        rows = [sel[gi * e + i:gi * e + i + 1] for i in range(e)]
        top2 = None
        for i in range(e):
            for k in range(i + 1, e):
                s = rows[i] + rows[k]
                top2 = s if top2 is None else jnp.maximum(top2, s)
        if best is None:
            best, best_v = jnp.zeros_like(top2, dtype=jnp.int32), top2
        else:
            upd = top2 > best_v
            best = jnp.where(upd, gi, best)
            best_v = jnp.where(upd, top2, best_v)

    def pick(a, i):
        out = a[i:i + 1]
        for gi in range(1, N_GROUPS):
            out = jnp.where(best == gi, a[gi * e + i:gi * e + i + 1], out)
        return out

    cs = [pick(sel, i) for i in range(e)]
    sc = [pick(scores, i) for i in range(e)]

    def argmax4(vals):
        idx, val = jnp.zeros_like(best), vals[0]
        for i in range(1, e):
            upd = vals[i] > val
            idx = jnp.where(upd, i, idx)
            val = jnp.where(upd, vals[i], val)
        return idx

    i1 = argmax4(cs)
    i2 = argmax4([jnp.where(i1 == i, -jnp.inf, cs[i]) for i in range(e)])

    def take(vals, idx):
        out = vals[0]
        for i in range(1, e):
            out = jnp.where(idx == i, vals[i], out)
        return out

    s1, s2 = take(sc, i1), take(sc, i2)
    tot = s1 + s2
    eid_ref[0] = jnp.concatenate([best * e + i1, best * e + i2], axis=0)
    gate_ref[0] = jnp.concatenate([s1 / tot, s2 / tot], axis=0)


def _out_proj(lhs, x, mod, gain, w, w_router, router_bias, n_ctx):
    b, t, d = x.shape
    tm = _tile_rows(n_ctx, t - n_ctx)
    nct = n_ctx // tm
    k = lhs.shape[2]
    ne = w_router.shape[1]
    tok = lambda i, j: (i, j, 0)
    c2 = lambda shape: pl.BlockSpec(shape, lambda i, j: (0, 0), pipeline_mode=pl.Buffered(1))
    return pl.pallas_call(
        _out_kernel, grid=(b, t // tm),
        in_specs=[pl.BlockSpec((1, tm, k), tok), pl.BlockSpec((1, tm, d), tok),
                  pl.BlockSpec((1, 1, 6, d), lambda i, j: (i, jnp.where(j < nct, 0, 1), 0, 0)),
                  c2((1, d)), c2((k, d)), c2((ne, d)), c2((ne, 1))],
        out_specs=[pl.BlockSpec((1, tm, d), tok), pl.BlockSpec((1, tm, d), tok),
                   pl.BlockSpec((1, 2, tm), lambda i, j: (i, 0, j)),
                   pl.BlockSpec((1, 2, tm), lambda i, j: (i, 0, j))],
        out_shape=[jax.ShapeDtypeStruct((b, t, d), F32), jax.ShapeDtypeStruct((b, t, d), BF16),
                   jax.ShapeDtypeStruct((b, 2, t), jnp.int32), jax.ShapeDtypeStruct((b, 2, t), F32)],
        compiler_params=_cparams(("parallel", "parallel")), name="out_proj_router",
    )(lhs, x, mod, gain.reshape(1, d), w, w_router.T.astype(BF16), router_bias.reshape(ne, 1))


def _expert_kernel(it_ref, ie_ref, fl_ref, lo_ref, hi_ref, ni_ref, x_ref, wg_ref, wu_ref, wd_ref, o_ref,
                   wg_s, wu_s, wd_s):
    j = pl.program_id(0)
    tm = x_ref.shape[0]

    @pl.when(j < ni_ref[0])
    def _():
        @pl.when((fl_ref[j] & 1) != 0)
        def _():
            wg_s[...] = wg_ref[0, 0].astype(BF16)
            wu_s[...] = wu_ref[0, 0].astype(BF16)
            wd_s[...] = wd_ref[0, 0].astype(BF16)

        x = x_ref[...]
        a = jnp.dot(x, wg_s[...], preferred_element_type=F32)
        u = jnp.dot(x, wu_s[...], preferred_element_type=F32)
        he = (a * _sigmoid(a) * u).astype(BF16)
        y = jnp.dot(he, wd_s[...], preferred_element_type=F32).astype(BF16)
        row = it_ref[j] * tm + lax.broadcasted_iota(jnp.int32, (tm, 1), 0)
        mine = jnp.logical_and(row >= lo_ref[j], row < hi_ref[j])

        @pl.when((fl_ref[j] & 2) != 0)
        def _():
            o_ref[...] = jnp.where(mine, y, jnp.zeros_like(y))

        @pl.when((fl_ref[j] & 2) == 0)
        def _():
            o_ref[...] = jnp.where(mine, y, o_ref[...])


def _experts(x_sorted, items, layer, w_gate, w_up, w_down, tm):
    p, d = x_sorted.shape
    f = w_gate.shape[3]
    n_items = items[0].shape[0]
    wspec = lambda shape: pl.BlockSpec((1, 1) + shape, lambda j, it, ie, *_: (layer, ie[j], 0, 0),
                                       pipeline_mode=pl.Buffered(1))
    grid_spec = pltpu.PrefetchScalarGridSpec(
        num_scalar_prefetch=6, grid=(n_items,),
        in_specs=[pl.BlockSpec((tm, d), lambda j, it, *_: (it[j], 0)), wspec((d, f)), wspec((d, f)), wspec((f, d))],
        out_specs=pl.BlockSpec((tm, d), lambda j, it, *_: (it[j], 0)),
        scratch_shapes=[pltpu.VMEM((d, f), BF16), pltpu.VMEM((d, f), BF16), pltpu.VMEM((f, d), BF16)])
    return pl.pallas_call(
        _expert_kernel, grid_spec=grid_spec,
        out_shape=jax.ShapeDtypeStruct((p, d), BF16),
        compiler_params=_cparams(("arbitrary",)), name="moe_experts",
    )(*items, x_sorted, w_gate, w_up, w_down)


def _moe(h2, eid, gates, layer, w_gate, w_up, w_down, tm):
    b, t, d = h2.shape
    n = b * t
    ne = w_gate.shape[1]
    e_flat = eid.transpose(1, 0, 2).reshape(n * 2)
    slot = jnp.arange(2 * n, dtype=jnp.int32)
    slot_bits = (2 * n - 1).bit_length()
    assert ne << slot_bits < 2 ** 31
    key = lax.sort(e_flat * (1 << slot_bits) + slot)
    sorted_e = key >> slot_bits
    order = key & ((1 << slot_bits) - 1)
    _, pos = lax.sort((order, slot), num_keys=1)
    bounds = jnp.sum(sorted_e[None, :] < jnp.arange(ne + 1, dtype=jnp.int32)[:, None], axis=1).astype(jnp.int32)
    lo_t = bounds[:-1] // tm
    n_e = jnp.where(bounds[1:] > bounds[:-1], (bounds[1:] - 1) // tm - lo_t + 1, 0)
    item_end = jnp.cumsum(n_e)
    item_start = item_end - n_e
    n_items = item_end[-1]
    max_items = (2 * n) // tm + ne - 1
    jj = jnp.minimum(jnp.arange(max_items, dtype=jnp.int32), n_items - 1)
    ie = jnp.minimum(jnp.sum(item_end[None, :] <= jj[:, None], axis=1), ne - 1).astype(jnp.int32)
    onehot = ie[:, None] == jnp.arange(ne, dtype=jnp.int32)[None, :]
    pick = lambda tab: jnp.sum(jnp.where(onehot, tab[None, :], 0), axis=1).astype(jnp.int32)
    it = pick(lo_t) + jj - pick(item_start)
    first = jnp.arange(max_items) == 0
    new_e = jnp.logical_or(first, ie != jnp.roll(ie, 1))
    new_t = jnp.logical_or(first, it != jnp.roll(it, 1))
    flags = new_e.astype(jnp.int32) + 2 * new_t.astype(jnp.int32)
    items = (it, ie, flags, pick(bounds[:-1]), pick(bounds[1:]), n_items.reshape(1).astype(jnp.int32))
    x_sorted = jnp.take(h2.reshape(n, d), jnp.where(order >= n, order - n, order), axis=0, mode="clip")
    y_sorted = _experts(x_sorted, items, layer, w_gate, w_up, w_down, tm)
    y1 = jnp.take(y_sorted, pos[:n], axis=0, mode="clip").reshape(b, t, d)
    y2 = jnp.take(y_sorted, pos[n:], axis=0, mode="clip").reshape(b, t, d)
    return y1, y2, gates.transpose(0, 2, 1)


def _final_kernel(x_ref, y1_ref, y2_ref, gt_ref, mprev_ref, g_ref, o_ref):
    x = _residual_in(x_ref, (y1_ref, y2_ref, gt_ref, mprev_ref))
    o_ref[0] = x * lax.rsqrt(jnp.mean(x * x, axis=-1, keepdims=True) + RMS_EPS) * g_ref[...]


def _final(x, delta, mod_prev, gain, n_ctx):
    b, t, d = x.shape
    tm = _tile_rows(n_ctx, t - n_ctx)
    nct = n_ctx // tm
    tok = pl.BlockSpec((1, tm, d), lambda i, j: (i, j + nct, 0))
    return pl.pallas_call(
        _final_kernel, grid=(b, (t - n_ctx) // tm),
        in_specs=[tok, tok, tok, pl.BlockSpec((1, tm, 2), lambda i, j: (i, j + nct, 0)),
                  pl.BlockSpec((1, 1, 6, d), lambda i, j: (i, 1, 0, 0)),
                  pl.BlockSpec((1, d), lambda i, j: (0, 0))],
        out_specs=pl.BlockSpec((1, tm, d), lambda i, j: (i, j, 0)),
        out_shape=jax.ShapeDtypeStruct((b, t - n_ctx, d), F32),
        compiler_params=_cparams(("parallel", "parallel")), name="final_norm",
    )(x, *delta, mod_prev, gain.reshape(1, d))


def kernel(x, c, ctx, c_ctx, norm1_g, norm2_g, w_mod, b_mod, w_in_e, mu_prev, mu_next, w0, w2, a0, a2, g2,
           k_k, k_a, r_k, lnx_g, lnx_b, conv_b_w, conv_b_bias, lnb_g, lnb_b, w_out_e, w_in_o, conv_c_w,
           w_out_o, w_router, router_bias, w_gate, w_up, w_down, final_g):
    b, s, d = x.shape
    n_ctx = ctx.shape[1]
    depth = w_mod.shape[0]
    tm = _tile_rows(n_ctx, s)
    aw = w0.shape[2]
    a_proj = mu_prev.shape[1]

    xa = (ctx, x)
    rows = ((b + 1 + 7) // 8) * 8
    cv = jnp.zeros((rows, d), F32).at[:b].set(c).at[b].set(c_ctx)
    mods_all = _adaln(cv, w_mod, b_mod).reshape(depth, rows, 6, d)
    mods = [jnp.stack([jnp.broadcast_to(mods_all[i, b], (b, 6, d)), mods_all[i, :b]], axis=1)
            for i in range(depth)]

    delta = None
    for i in range(depth):
        j = i // 2
        mod_prev = mods[i - 1] if i else None
        if i % 2 == 0:
            xa, z = _in_proj("even", xa, delta, mod_prev, mods[i], norm1_g[i], w_in_e[j].astype(BF16), None,
                             n_ctx)
            r, v, kkn, g, ld, kd, icl = _prepare(z, mu_prev[j], mu_next[j], w0[j], w2[j], a0[j], a2[j], g2[j],
                                                 k_k[j], k_a[j], n_ctx)
            y = _scan(r, v, kkn, ld, kd, icl, n_ctx)
            ub = _conformer_conv(z, a_proj, conv_b_w[j], conv_b_bias[j], n_ctx)
            lhs = _merge(y, r, v, kd, g, ub, r_k[j].reshape(aw), lnx_g[j], lnx_b[j], lnb_g[j], lnb_b[j], n_ctx)
            w_out = w_out_e[j]
        else:
            xa, lhs = _in_proj("odd", xa, delta, mod_prev, mods[i], norm1_g[i], w_in_o[j].astype(BF16),
                               conv_c_w[j], n_ctx)
            w_out = w_out_o[j]
        xa, h2, eid, gates = _out_proj(lhs, xa, mods[i], norm2_g[i], w_out.astype(BF16), w_router, router_bias,
                                       n_ctx)
        delta = _moe(h2, eid, gates, i, w_gate, w_up, w_down, tm)
    return _final(xa, delta, mods[depth - 1], final_g, n_ctx)
```
